```python
import math
import jax, jax.numpy as jnp
from jax import lax
import numpy as np


D_MODEL = 1024
BATCH = 32
SEQ = 256
DEPTH = 2
DEC_BATCH = 4
DEC_SEQ = 2048
PAST_LEN = 512

GRID_W = 64
N_MIXERS = 2
N_NA_LAYERS = (DEPTH + 1) // 2
N_SSM_LAYERS = DEPTH // 2
RMS_EPS = 1e-6
MASK_VALUE = -1e30
NA_HEADS = 16
NA_HEAD_DIM = D_MODEL // NA_HEADS
NA_WIN_H = 8
NA_WIN_W = 16
CTX_Q_BLOCK = 128
SSM_D_INNER = 2 * D_MODEL
SSM_HEAD_DIM = 64
SSM_HEADS = SSM_D_INNER // SSM_HEAD_DIM
SSM_GROUPS = 8
SSM_HPG = SSM_HEADS // SSM_GROUPS
SSM_STATE = 128
SSM_CONV = 4
SSM_CHUNK = 128
SSM_CONV_DIM = SSM_D_INNER + 2 * SSM_GROUPS * SSM_STATE
SSM_IN_DIM = SSM_D_INNER + SSM_CONV_DIM + 2 * SSM_HEADS
PEER_HEADS = 8
PEER_N_KEYS = 128
PEER_EXPERTS = PEER_N_KEYS * PEER_N_KEYS
PEER_TOPK = 16
PEER_QUERY_DIM = 256
PEER_HALF = PEER_QUERY_DIM // 2
PEER_TOKEN_BLOCK = 128

kernel_name = 'hybrid_na_ssd_peer_diffusion_step'


def _rmsnorm(x, w):
    xf = x.astype(jnp.float32)
    xf = xf * lax.rsqrt(jnp.mean(xf * xf, axis=-1, keepdims=True) + RMS_EPS)
    return (xf * w.astype(jnp.float32)).astype(x.dtype)


def _modulation(cond, w, b):
    m = jax.nn.silu(cond) @ w + b
    return m.reshape(cond.shape[0], 1, 6, D_MODEL)


def _modulate(x, shift, scale):
    return x * (1.0 + scale) + shift


def _context_self_attention(q, k, v):
    B, L, H, Dh = q.shape
    nblk = L // CTX_Q_BLOCK
    qb = jnp.moveaxis(q.reshape(B, nblk, CTX_Q_BLOCK, H, Dh), 1, 0)
    scale = Dh ** -0.5

    def one_block(qi):
        s = jnp.einsum('bqhd,bkhd->bhqk', qi, k).astype(jnp.float32) * scale
        p = jax.nn.softmax(s, axis=-1).astype(v.dtype)
        return jnp.einsum('bhqk,bkhd->bqhd', p, v)

    o = lax.map(one_block, qb)
    return jnp.moveaxis(o, 0, 1).reshape(B, L, H * Dh)


def _na_context(h, w_qkv, w_o):
    B, L, _ = h.shape
    qkv = (h @ w_qkv).reshape(B, L, 3, NA_HEADS, NA_HEAD_DIM)
    q, k, v = qkv[:, :, 0], qkv[:, :, 1], qkv[:, :, 2]
    o = _context_self_attention(q, k, v)
    return o @ w_o, k, v


def _na_latent(h, ctx_k, ctx_v, w_qkv, w_o, rpb):
    B, N, _ = h.shape
    rows = N // GRID_W
    kh = min(NA_WIN_H, rows)
    qkv = (h @ w_qkv).reshape(B, rows, GRID_W, 3, NA_HEADS, NA_HEAD_DIM)
    q, k, v = qkv[:, :, :, 0], qkv[:, :, :, 1], qkv[:, :, :, 2]
    r = jnp.arange(rows)
    row_idx = jnp.clip(r - kh // 2, 0, rows - kh)[:, None] + jnp.arange(kh)[None, :]
    k_rows = k[:, row_idx]
    v_rows = v[:, row_idx]
    cols = jnp.arange(GRID_W)
    col_start = jnp.clip(cols - NA_WIN_W // 2, 0, GRID_W - NA_WIN_W)
    col_ok = (cols[None, :] >= col_start[:, None]) & (cols[None, :] < col_start[:, None] + NA_WIN_W)
    dr = row_idx - r[:, None] + (NA_WIN_H - 1)
    dc = jnp.clip(cols[None, :] - cols[:, None], -(NA_WIN_W - 1), NA_WIN_W - 1) + (NA_WIN_W - 1)
    bias = rpb[:, dr[:, None, :, None], dc[None, :, None, :]].astype(jnp.float32)
    scale = NA_HEAD_DIM ** -0.5
    s_loc = jnp.einsum('brqhd,brkchd->bhrqkc', q, k_rows).astype(jnp.float32) * scale + bias[None]
    s_loc = jnp.where(col_ok[:, None, :], s_loc, MASK_VALUE)
    s_ctx = jnp.einsum('brqhd,bmhd->bhrqm', q, ctx_k).astype(jnp.float32) * scale
    n_loc = kh * GRID_W
    s = jnp.concatenate([s_loc.reshape(B, NA_HEADS, rows, GRID_W, n_loc), s_ctx], axis=-1)
    p = jax.nn.softmax(s, axis=-1).astype(v.dtype)
    p_loc = p[..., :n_loc].reshape(B, NA_HEADS, rows, GRID_W, kh, GRID_W)
    o = (jnp.einsum('bhrqkc,brkchd->brqhd', p_loc, v_rows)
         + jnp.einsum('bhrqm,bmhd->brqhd', p[..., n_loc:], ctx_v))
    return o.reshape(B, N, D_MODEL) @ w_o


def _centred_depthwise_conv(x, w, b):
    pad_lo = (SSM_CONV - 1) // 2
    pad_hi = SSM_CONV - 1 - pad_lo
    y = lax.conv_general_dilated(x, w.reshape(SSM_CONV, 1, -1), window_strides=(1,),
                                 padding=[(pad_lo, pad_hi)],
                                 dimension_numbers=('NWC', 'WIO', 'NWC'),
                                 feature_group_count=x.shape[-1])
    return y + b


def _ssd_scan(x, dt, a, bm, cm, h0):
    B, L = x.shape[:2]
    nc = L // SSM_CHUNK
    f32 = jnp.float32
    xs = (x.astype(f32) * dt[..., None]).reshape(B, nc, SSM_CHUNK, SSM_GROUPS, SSM_HPG, SSM_HEAD_DIM)
    da = (dt * a).reshape(B, nc, SSM_CHUNK, SSM_GROUPS, SSM_HPG)
    bc = bm.astype(f32).reshape(B, nc, SSM_CHUNK, SSM_GROUPS, SSM_STATE)
    cc = cm.astype(f32).reshape(B, nc, SSM_CHUNK, SSM_GROUPS, SSM_STATE)
    a_cum = jnp.cumsum(da, axis=2)
    seg = a_cum[:, :, :, None] - a_cum[:, :, None, :]
    tri = (jnp.arange(SSM_CHUNK)[:, None] >= jnp.arange(SSM_CHUNK)[None, :])[:, :, None, None]
    decay_in = jnp.exp(jnp.where(tri, seg, -jnp.inf))
    w_ts = jnp.einsum('bctgn,bcsgn->bctsg', cc, bc)[..., None] * decay_in
    y_diag = jnp.einsum('bctsgr,bcsgrp->bctgrp', w_ts, xs)
    decay_to_end = jnp.exp(a_cum[:, :, -1:] - a_cum)
    chunk_states = jnp.einsum('bcsgn,bcsgr,bcsgrp->bcgrpn', bc, decay_to_end, xs)
    chunk_decay = jnp.exp(a_cum[:, :, -1])

    def step(hc, inp):
        st, dec = inp
        return dec[..., None, None] * hc + st, hc

    h_last, h_prev = lax.scan(step, h0.astype(f32),
                              (jnp.moveaxis(chunk_states, 1, 0), jnp.moveaxis(chunk_decay, 1, 0)))
    h_prev = jnp.moveaxis(h_prev, 0, 1)
    y_off = jnp.einsum('bctgn,bcgrpn,bctgr->bctgrp', cc, h_prev, jnp.exp(a_cum))
    y = (y_diag + y_off).reshape(B, L, SSM_GROUPS, SSM_HPG, SSM_HEAD_DIM)
    return y.astype(x.dtype), h_last.astype(h0.dtype)


def _grouped_rmsnorm(y, w):
    yf = y.astype(jnp.float32).reshape(*y.shape[:-1], SSM_GROUPS, SSM_D_INNER // SSM_GROUPS)
    yf = yf * lax.rsqrt(jnp.mean(yf * yf, axis=-1, keepdims=True) + RMS_EPS)
    return (yf.reshape(y.shape) * w.astype(jnp.float32)).astype(y.dtype)


def _ssd_mixer(h, state0, w_in, conv_w, conv_b, dt_bias, a_log, d_skip, norm_w, w_out):
    B, L, _ = h.shape
    proj = h @ w_in
    z = proj[..., :SSM_D_INNER]
    xbc = proj[..., SSM_D_INNER:SSM_D_INNER + SSM_CONV_DIM]
    dt_raw = proj[..., SSM_D_INNER + SSM_CONV_DIM:].reshape(B, L, 2, SSM_GROUPS, SSM_HPG)
    xbc = jax.nn.silu(_centred_depthwise_conv(xbc, conv_w, conv_b))
    gn = SSM_GROUPS * SSM_STATE
    xs = xbc[..., :SSM_D_INNER].reshape(B, L, SSM_GROUPS, SSM_HPG, SSM_HEAD_DIM)
    bm = xbc[..., SSM_D_INNER:SSM_D_INNER + gn].reshape(B, L, SSM_GROUPS, SSM_STATE)
    cm = xbc[..., SSM_D_INNER + gn:].reshape(B, L, SSM_GROUPS, SSM_STATE)
    dt = jax.nn.softplus(dt_raw.astype(jnp.float32)
                         + dt_bias.astype(jnp.float32).reshape(2, SSM_GROUPS, SSM_HPG))
    a = -jnp.exp(a_log.astype(jnp.float32)).reshape(2, SSM_GROUPS, SSM_HPG)
    h0 = state0.reshape(B, 2, SSM_GROUPS, SSM_HPG, SSM_HEAD_DIM, SSM_STATE)
    y_f, h_f = _ssd_scan(xs, dt[:, :, 0], a[0], bm, cm, h0[:, 0])
    flip = lambda t: jnp.flip(t, axis=1)
    y_b, h_b = _ssd_scan(flip(xs), flip(dt[:, :, 1]), a[1], flip(bm), flip(cm), h0[:, 1])
    y = y_f + flip(y_b) + d_skip.reshape(SSM_GROUPS, SSM_HPG)[..., None] * xs
    y = _grouped_rmsnorm(y.reshape(B, L, SSM_D_INNER) * jax.nn.silu(z), norm_w)
    final = jnp.stack([h_f, h_b], axis=1).reshape(B, 2, SSM_HEADS, SSM_HEAD_DIM, SSM_STATE)
    return y @ w_out, final


def _peer(h, w_q, sub_keys, u_tab, v_tab):
    lead = h.shape[:-1]
    hf = h.reshape(-1, D_MODEL)
    T = hf.shape[0]
    q = (hf @ w_q).reshape(T, PEER_HEADS, 2, PEER_HALF)
    s = jnp.einsum('thzc,hznc->thzn', q, sub_keys).astype(jnp.float32)
    s1, i1 = lax.top_k(s[:, :, 0], PEER_TOPK)
    s2, i2 = lax.top_k(s[:, :, 1], PEER_TOPK)
    cand = (s1[..., :, None] + s2[..., None, :]).reshape(T, PEER_HEADS, PEER_TOPK * PEER_TOPK)
    best, j = lax.top_k(cand, PEER_TOPK)
    e1 = jnp.take_along_axis(i1, j // PEER_TOPK, axis=-1)
    e2 = jnp.take_along_axis(i2, j % PEER_TOPK, axis=-1)
    idx = e1 * PEER_N_KEYS + e2
    g = jax.nn.softmax(best, axis=-1).astype(h.dtype)
    nb = T // PEER_TOKEN_BLOCK

    def block(args):
        xb, ib, gb = args
        act = jax.nn.gelu(jnp.einsum('td,thkd->thk', xb, u_tab[ib])) * gb
        return jnp.einsum('thk,thkd->td', act, v_tab[ib])

    out = lax.map(block, (hf.reshape(nb, PEER_TOKEN_BLOCK, D_MODEL),
                          idx.reshape(nb, PEER_TOKEN_BLOCK, PEER_HEADS, PEER_TOPK),
                          g.reshape(nb, PEER_TOKEN_BLOCK, PEER_HEADS, PEER_TOPK)))
    return out.reshape(*lead, D_MODEL)


def setup_inputs(seed: int = 0) -> dict:
    key = jax.random.key(seed)
    k = jax.random.split(key, 32)
    f32 = jnp.float32
    inv_d = D_MODEL ** -0.5

    def nrm(kk, shape, scale):
        return jax.random.normal(kk, shape, f32) * scale

    dt0 = jnp.exp(jax.random.uniform(k[20], (N_SSM_LAYERS, 2, SSM_HEADS), f32,
                                     math.log(1e-3), math.log(1e-1)))
    return {
        'x_prompt': nrm(k[0], (BATCH, SEQ, D_MODEL), 1.0),
        'x_sample': nrm(k[1], (DEC_BATCH, DEC_SEQ, D_MODEL), 1.0),
        'cache_k': nrm(k[2], (DEC_BATCH, N_NA_LAYERS, PAST_LEN, NA_HEADS, NA_HEAD_DIM), 1.0),
        'cache_v': nrm(k[3], (DEC_BATCH, N_NA_LAYERS, PAST_LEN, NA_HEADS, NA_HEAD_DIM), 1.0),
        'state_ssm': nrm(k[4], (DEC_BATCH, N_SSM_LAYERS, 2, SSM_HEADS, SSM_HEAD_DIM, SSM_STATE), 0.1),
        'c': nrm(k[5], (DEC_BATCH, D_MODEL), 1.0),
        'c_ctx': nrm(k[6], (D_MODEL,), 1.0),
        'ada_w': nrm(k[7], (DEPTH, D_MODEL, 6 * D_MODEL), inv_d),
        'ada_b': nrm(k[8], (DEPTH, 6 * D_MODEL), 0.01),
        'norm1_w': 1.0 + nrm(k[9], (DEPTH, D_MODEL), 0.01),
        'norm2_w': 1.0 + nrm(k[10], (DEPTH, D_MODEL), 0.01),
        'final_norm_w': 1.0 + nrm(k[11], (D_MODEL,), 0.01),
        'na_w_qkv': nrm(k[12], (N_NA_LAYERS, D_MODEL, 3 * D_MODEL), inv_d),
        'na_w_o': nrm(k[13], (N_NA_LAYERS, D_MODEL, D_MODEL), inv_d),
        'na_rpb': nrm(k[14], (N_NA_LAYERS, NA_HEADS, 2 * NA_WIN_H - 1, 2 * NA_WIN_W - 1), 0.1),
        'ssm_w_in': nrm(k[15], (N_SSM_LAYERS, D_MODEL, SSM_IN_DIM), inv_d),
        'ssm_conv_w': nrm(k[16], (N_SSM_LAYERS, SSM_CONV, SSM_CONV_DIM), SSM_CONV ** -0.5),
        'ssm_conv_b': nrm(k[17], (N_SSM_LAYERS, SSM_CONV_DIM), 0.01),
        'ssm_dt_bias': dt0 + jnp.log(-jnp.expm1(-dt0)),
        'ssm_a_log': jnp.log(jax.random.uniform(k[21], (N_SSM_LAYERS, 2, SSM_HEADS), f32, 1.0, 16.0)),
        'ssm_d': 1.0 + nrm(k[22], (N_SSM_LAYERS, SSM_HEADS), 0.01),
        'ssm_norm_w': 1.0 + nrm(k[23], (N_SSM_LAYERS, SSM_D_INNER), 0.01),
        'ssm_w_out': nrm(k[24], (N_SSM_LAYERS, SSM_D_INNER, D_MODEL), SSM_D_INNER ** -0.5),
        'peer_w_q': nrm(k[25], (DEPTH, D_MODEL, PEER_HEADS * PEER_QUERY_DIM), inv_d),
        'peer_keys': nrm(k[26], (DEPTH, PEER_HEADS, 2, PEER_N_KEYS, PEER_HALF), PEER_HALF ** -0.5),
        'peer_u': nrm(k[27], (DEPTH, PEER_EXPERTS, D_MODEL), inv_d),
        'peer_v': nrm(k[28], (DEPTH, PEER_EXPERTS, D_MODEL), PEER_HEADS ** -0.5),
    }


def reference(x_prompt, x_sample, cache_k, cache_v, state_ssm, c, c_ctx,
              ada_w, ada_b, norm1_w, norm2_w, final_norm_w,
              na_w_qkv, na_w_o, na_rpb,
              ssm_w_in, ssm_conv_w, ssm_conv_b, ssm_dt_bias, ssm_a_log, ssm_d, ssm_norm_w, ssm_w_out,
              peer_w_q, peer_keys, peer_u, peer_v):
    xp, xs = x_prompt, x_sample
    new_k, new_v, new_s = [], [], []
    for i in range(DEPTH):
        j = i // N_MIXERS
        mp = _modulation(c_ctx[None, :], ada_w[i], ada_b[i])
        ms = _modulation(c, ada_w[i], ada_b[i])
        hp = _modulate(_rmsnorm(xp, norm1_w[i]), mp[:, :, 0], mp[:, :, 1])
        hs = _modulate(_rmsnorm(xs, norm1_w[i]), ms[:, :, 0], ms[:, :, 1])
        if i % N_MIXERS == 0:
            op, k_ctx, v_ctx = _na_context(hp, na_w_qkv[j], na_w_o[j])
            os_ = _na_latent(hs, cache_k[:, j], cache_v[:, j], na_w_qkv[j], na_w_o[j], na_rpb[j])
            new_k.append(k_ctx)
            new_v.append(v_ctx)
        else:
            zero_state = jnp.zeros((xp.shape[0], 2, SSM_HEADS, SSM_HEAD_DIM, SSM_STATE), xp.dtype)
            op, st_ctx = _ssd_mixer(hp, zero_state, ssm_w_in[j], ssm_conv_w[j], ssm_conv_b[j],
                                    ssm_dt_bias[j], ssm_a_log[j], ssm_d[j], ssm_norm_w[j], ssm_w_out[j])
            os_, _ = _ssd_mixer(hs, state_ssm[:, j], ssm_w_in[j], ssm_conv_w[j], ssm_conv_b[j],
                                ssm_dt_bias[j], ssm_a_log[j], ssm_d[j], ssm_norm_w[j], ssm_w_out[j])
            new_s.append(st_ctx)
        xp = xp + mp[:, :, 2] * op
        xs = xs + ms[:, :, 2] * os_
        fp = _modulate(_rmsnorm(xp, norm2_w[i]), mp[:, :, 3], mp[:, :, 4])
        fs = _modulate(_rmsnorm(xs, norm2_w[i]), ms[:, :, 3], ms[:, :, 4])
        xp = xp + mp[:, :, 5] * _peer(fp, peer_w_q[i], peer_keys[i], peer_u[i], peer_v[i])
        xs = xs + ms[:, :, 5] * _peer(fs, peer_w_q[i], peer_keys[i], peer_u[i], peer_v[i])
    y_prompt = _rmsnorm(xp, final_norm_w)
    y_sample = _rmsnorm(xs, final_norm_w)
    new_cache_k = jnp.stack(new_k, axis=1)
    new_cache_v = jnp.stack(new_v, axis=1)
    new_state_ssm = jnp.stack(new_s, axis=1)
    return (y_prompt, y_sample, new_cache_k, new_cache_v, new_state_ssm)
```

```python
import functools
import math

import jax
import jax.numpy as jnp
from jax import lax
from jax.experimental import pallas as pl
from jax.experimental.pallas import tpu as pltpu

F32 = jnp.float32
BF16 = jnp.bfloat16
HIGHEST = lax.Precision.HIGHEST

D_MODEL = 1024
DEPTH = 2
GRID_W = 64
RMS_EPS = 1e-6
MASK_VALUE = -1e30
NA_HEADS = 16
NA_HEAD_DIM = 64
NA_WIN_H = 8
NA_WIN_W = 16
SSM_D_INNER = 2048
SSM_HEAD_DIM = 64
SSM_HEADS = 32
SSM_GROUPS = 8
SSM_HPG = 4
SSM_STATE = 128
SSM_CONV = 4
SSM_CHUNK = 128
SSM_CONV_DIM = 4096
SSM_DT_PAD = 128
SSM_PROJ_DIM = SSM_D_INNER + SSM_CONV_DIM + SSM_DT_PAD
PEER_HEADS = 8
PEER_N_KEYS = 128
PEER_EXPERTS = PEER_N_KEYS * PEER_N_KEYS
PEER_TOPK = 16
PEER_QUERY_DIM = 256
PEER_HALF = 128

VMEM_LIMIT = 56 * 1024 * 1024


def _params(sem, vmem=VMEM_LIMIT):
    return pltpu.CompilerParams(dimension_semantics=sem, vmem_limit_bytes=vmem)


def _nt_dot(a, b):
    return lax.dot_general(a, b, (((1,), (1,)), ((), ())), preferred_element_type=F32)


def _tn_dot(a, b):
    return lax.dot_general(a, b, (((0,), (0,)), ((), ())), preferred_element_type=F32)


def _silu(x):
    return x / (1.0 + jnp.exp(-x))


def _softplus(x):
    return jnp.maximum(x, 0.0) + jnp.log1p(jnp.exp(-jnp.abs(x)))


def _gelu_tanh(x):
    c = math.sqrt(2.0 / math.pi)
    return 0.5 * x * (1.0 + jnp.tanh(c * (x + 0.044715 * (x * x * x))))


def _mod_kernel(cond_ref, w_ref, b_ref, o_ref):
    s = _silu(cond_ref[...])
    o_ref[0] = jnp.dot(s, w_ref[0], preferred_element_type=F32, precision=HIGHEST) + b_ref[0]


def _modulation(cond, ada_w, ada_b):
    depth, d, n = ada_w.shape
    tn = 1536
    return pl.pallas_call(
        _mod_kernel,
        grid=(depth, n // tn),
        in_specs=[pl.BlockSpec((8, d), lambda i, j: (0, 0)),
                  pl.BlockSpec((1, d, tn), lambda i, j: (i, 0, j)),
                  pl.BlockSpec((1, 1, tn), lambda i, j: (i, 0, j))],
        out_specs=pl.BlockSpec((1, 8, tn), lambda i, j: (i, 0, j)),
        out_shape=jax.ShapeDtypeStruct((depth, 8, n), F32),
        compiler_params=_params(("arbitrary", "arbitrary")),
        name="adaln_modulation",
    )(cond, ada_w, ada_b.reshape(depth, 1, n))


def _nmm_kernel(x_ref, nw_ref, mod_ref, w_ref, *refs, shift_idx, n_out):
    out_refs, h_scr = refs[:n_out], refs[n_out]
    j = pl.program_id(1)

    @pl.when(j == 0)
    def _():
        x = x_ref[...]
        ms = jnp.mean(x * x, axis=-1, keepdims=True)
        xn = x * lax.rsqrt(ms + RMS_EPS) * nw_ref[...]
        m = mod_ref[0]
        h = xn * (1.0 + m[shift_idx + 1:shift_idx + 2]) + m[shift_idx:shift_idx + 1]
        h_scr[...] = h.astype(BF16)

    acc = jnp.dot(h_scr[...], w_ref[...], preferred_element_type=F32)
    return acc, out_refs, h_scr, j


def _nmm_kernel_main(x_ref, nw_ref, mod_ref, w_ref, *refs, shift_idx, out_dtypes, emit_h):
    n_out = len(out_dtypes) + (1 if emit_h else 0)
    acc, out_refs, h_scr, j = _nmm_kernel(x_ref, nw_ref, mod_ref, w_ref, *refs,
                                          shift_idx=shift_idx, n_out=n_out)
    for r, dt in zip(out_refs, out_dtypes):
        r[...] = acc.astype(dt)
    if emit_h:
        @pl.when(j == 0)
        def _():
            out_refs[-1][...] = h_scr[...]


def _norm_mod_matmul(x, norm_w, mods, w_bf16, *, shift_idx, cond_of_block, tb, tn,
                     out_dtypes, emit_h=False, name="norm_mod_matmul"):
    t, d = x.shape
    n = w_bf16.shape[1]
    assert t % tb == 0 and n % tn == 0
    out_shape = [jax.ShapeDtypeStruct((t, n), dt) for dt in out_dtypes]
    out_specs = [pl.BlockSpec((tb, tn), lambda i, j: (i, j)) for _ in out_dtypes]
    if emit_h:
        out_shape.append(jax.ShapeDtypeStruct((t, d), BF16))
        out_specs.append(pl.BlockSpec((tb, d), lambda i, j: (i, 0)))
    kern = functools.partial(_nmm_kernel_main, shift_idx=shift_idx,
                             out_dtypes=tuple(out_dtypes), emit_h=emit_h)
    return pl.pallas_call(
        kern,
        grid=(t // tb, n // tn),
        in_specs=[pl.BlockSpec((tb, d), lambda i, j: (i, 0)),
                  pl.BlockSpec((1, d), lambda i, j: (0, 0)),
                  pl.BlockSpec((1, 6, d), lambda i, j: (cond_of_block(i, tb), 0, 0)),
                  pl.BlockSpec((d, tn), lambda i, j: (0, j))],
        out_specs=out_specs,
        out_shape=out_shape,
        scratch_shapes=[pltpu.VMEM((tb, d), BF16)],
        compiler_params=_params(("arbitrary", "arbitrary")),
        name=name,
    )(x, norm_w.reshape(1, d), mods, w_bf16)


def _mm_res_kernel(a_ref, w_ref, x_ref, mod_ref, o_ref, *, gate_idx):
    acc = jnp.dot(a_ref[...], w_ref[...], preferred_element_type=F32)
    o_ref[...] = x_ref[...] + mod_ref[0][gate_idx:gate_idx + 1] * acc


def _matmul_residual(a, w_bf16, x, mods, *, gate_idx, cond_of_block, tb, name="matmul_residual"):
    t, k = a.shape
    d = w_bf16.shape[1]
    return pl.pallas_call(
        functools.partial(_mm_res_kernel, gate_idx=gate_idx),
        grid=(t // tb,),
        in_specs=[pl.BlockSpec((tb, k), lambda i: (i, 0)),
                  pl.BlockSpec((k, d), lambda i: (0, 0)),
                  pl.BlockSpec((tb, d), lambda i: (i, 0)),
                  pl.BlockSpec((1, 6, d), lambda i: (cond_of_block(i, tb), 0, 0))],
        out_specs=pl.BlockSpec((tb, d), lambda i: (i, 0)),
        out_shape=jax.ShapeDtypeStruct((t, d), F32),
        compiler_params=_params(("arbitrary",)),
        name=name,
    )(a, w_bf16, x, mods)


def _ctx_attn_kernel(q_ref, k_ref, v_ref, o_ref):
    scale = NA_HEAD_DIM ** -0.5
    for h in range(NA_HEADS):
        sl = slice(h * NA_HEAD_DIM, (h + 1) * NA_HEAD_DIM)
        s = _nt_dot(q_ref[:, sl], k_ref[:, sl]) * scale
        m = jnp.max(s, axis=-1, keepdims=True)
        p = jnp.exp(s - m)
        l = jnp.sum(p, axis=-1, keepdims=True)
        o = jnp.dot(p.astype(BF16), v_ref[:, sl], preferred_element_type=F32) / l
        o_ref[:, sl] = o.astype(o_ref.dtype)


def _context_attention(qkv_bf16, batch, seq):
    d = D_MODEL
    return pl.pallas_call(
        _ctx_attn_kernel,
        grid=(batch,),
        in_specs=[pl.BlockSpec((seq, d), lambda b: (b, 0)),
                  pl.BlockSpec((seq, d), lambda b: (b, 1)),
                  pl.BlockSpec((seq, d), lambda b: (b, 2))],
        out_specs=pl.BlockSpec((seq, d), lambda b: (b, 0)),
        out_shape=jax.ShapeDtypeStruct((batch * seq, d), BF16),
        compiler_params=_params(("arbitrary",)),
        name="context_attention",
    )(qkv_bf16, qkv_bf16, qkv_bf16)


def _na_window_start(r, rows):
    return jnp.clip(r - NA_WIN_H // 2, 0, rows - NA_WIN_H)


def _na_kernel(q_ref, k_ref, v_ref, ck_ref, cv_ref, bias_ref, o_ref, *, rows):
    scale = NA_HEAD_DIM ** -0.5
    r = pl.program_id(1)
    n_loc = NA_WIN_H * GRID_W
    start = pl.multiple_of(_na_window_start(r, rows) * GRID_W, GRID_W)
    for h in range(NA_HEADS):
        sl = slice(h * NA_HEAD_DIM, (h + 1) * NA_HEAD_DIM)
        q = q_ref[:, sl]
        k_loc = k_ref[pl.ds(start, n_loc), sl]
        v_loc = v_ref[pl.ds(start, n_loc), sl]
        bias = bias_ref[h]
        s_loc = jnp.where(bias > 0.5 * MASK_VALUE, _nt_dot(q, k_loc) * scale + bias, MASK_VALUE)
        s_ctx = _nt_dot(q, ck_ref[:, sl]) * scale
        m = jnp.maximum(jnp.max(s_loc, axis=-1, keepdims=True),
                        jnp.max(s_ctx, axis=-1, keepdims=True))
        p_loc = jnp.exp(s_loc - m)
        p_ctx = jnp.exp(s_ctx - m)
        l = jnp.sum(p_loc, axis=-1, keepdims=True) + jnp.sum(p_ctx, axis=-1, keepdims=True)
        o = (jnp.dot(p_loc.astype(BF16), v_loc, preferred_element_type=F32)
             + jnp.dot(p_ctx.astype(BF16), cv_ref[:, sl], preferred_element_type=F32)) / l
        o_ref[:, sl] = o.astype(o_ref.dtype)


def _na_bias_tables(rpb, rows):
    cols = jnp.arange(GRID_W)
    col_start = jnp.clip(cols - NA_WIN_W // 2, 0, GRID_W - NA_WIN_W)
    col_ok = (cols[None, :] >= col_start[:, None]) & (cols[None, :] < col_start[:, None] + NA_WIN_W)
    dc = jnp.clip(cols[None, :] - cols[:, None], -(NA_WIN_W - 1), NA_WIN_W - 1) + (NA_WIN_W - 1)
    full = rpb[:, :, dc].astype(F32)
    full = jnp.where(col_ok[None, None], full, MASK_VALUE)
    variants = []
    for v in range(NA_WIN_H):
        t = full[:, v:v + NA_WIN_H]
        variants.append(jnp.transpose(t, (0, 2, 1, 3)).reshape(NA_HEADS, GRID_W, NA_WIN_H * GRID_W))
    return jnp.stack(variants, axis=0)


def _na_latent_attention(qkv_bf16, ck_bf16, cv_bf16, bias_tab, batch, n_tok):
    d = D_MODEL
    rows = n_tok // GRID_W
    past = ck_bf16.shape[1]
    n_loc = NA_WIN_H * GRID_W

    def variant(b, r):
        return _na_window_start(r, rows) - r + (NA_WIN_H - 1)

    return pl.pallas_call(
        functools.partial(_na_kernel, rows=rows),
        grid=(batch, rows),
        in_specs=[pl.BlockSpec((GRID_W, d), lambda b, r: (b * rows + r, 0)),
                  pl.BlockSpec((n_tok, d), lambda b, r: (b, 1)),
                  pl.BlockSpec((n_tok, d), lambda b, r: (b, 2)),
                  pl.BlockSpec((None, past, d), lambda b, r: (b, 0, 0)),
                  pl.BlockSpec((None, past, d), lambda b, r: (b, 0, 0)),
                  pl.BlockSpec((None, NA_HEADS, GRID_W, n_loc), lambda b, r: (variant(b, r), 0, 0, 0))],
        out_specs=pl.BlockSpec((GRID_W, d), lambda b, r: (b * rows + r, 0)),
        out_shape=jax.ShapeDtypeStruct((batch * n_tok, d), BF16),
        compiler_params=_params(("arbitrary", "arbitrary")),
        name="neighbourhood_attention",
    )(qkv_bf16, qkv_bf16, qkv_bf16, ck_bf16, cv_bf16, bias_tab)


def _conv_silu_kernel(x_ref, w_ref, b_ref, o_ref):
    x = x_ref[...]
    n = x.shape[0]
    row = lax.broadcasted_iota(jnp.int32, x.shape, 0)
    w = w_ref[...]
    xm1 = jnp.where(row >= 1, pltpu.roll(x, 1, 0), 0.0)
    xp1 = jnp.where(row < n - 1, pltpu.roll(x, n - 1, 0), 0.0)
    xp2 = jnp.where(row < n - 2, pltpu.roll(x, n - 2, 0), 0.0)
    y = w[0:1] * xm1 + w[1:2] * x + w[2:3] * xp1 + w[3:4] * xp2 + b_ref[...]
    o_ref[...] = _silu(y)


def _conv_silu(proj, conv_w, conv_b, batch, seq):
    cb = 512
    c0 = SSM_D_INNER // cb
    return pl.pallas_call(
        _conv_silu_kernel,
        grid=(batch, SSM_CONV_DIM // cb),
        in_specs=[pl.BlockSpec((seq, cb), lambda b, j: (b, c0 + j)),
                  pl.BlockSpec((SSM_CONV, cb), lambda b, j: (0, j)),
                  pl.BlockSpec((1, cb), lambda b, j: (0, j))],
        out_specs=pl.BlockSpec((seq, cb), lambda b, j: (b, j)),
        out_shape=jax.ShapeDtypeStruct((batch * seq, SSM_CONV_DIM), F32),
        compiler_params=_params(("arbitrary", "arbitrary")),
        name="ssd_conv_silu",
    )(proj, conv_w, conv_b.reshape(1, SSM_CONV_DIM))


def _ssd_scan_kernel(*refs, direction, zero_init, want_state):
    it = iter(refs)
    x_ref, b_ref, c_ref, dt_ref, dtb_ref, alog_ref = (next(it) for _ in range(6))
    h0_ref = None if zero_init else next(it)
    y_ref = next(it)
    hl_ref = next(it) if want_state else None
    h_scr = next(it)

    q = SSM_CHUNK
    g = pl.program_id(1)
    c = pl.program_id(2)
    nc = pl.num_programs(2)

    @pl.when(c == 0)
    def _():
        if zero_init:
            h_scr[...] = jnp.zeros_like(h_scr)
        else:
            h_scr[...] = h0_ref[...]

    dt_all = _softplus(dt_ref[...] + dtb_ref[...])
    da_all = dt_all * (-jnp.exp(alog_ref[...]))
    ri = lax.broadcasted_iota(jnp.int32, (q, q), 0)
    ci = lax.broadcasted_iota(jnp.int32, (q, q), 1)
    first = direction * SSM_HEADS + g * SSM_HPG
    sel = jnp.where((ri == first + ci) & (ci < SSM_HPG), 1.0, 0.0).astype(F32)
    dt4 = jnp.dot(dt_all, sel, preferred_element_type=F32, precision=HIGHEST)
    da4 = jnp.dot(da_all, sel, preferred_element_type=F32, precision=HIGHEST)
    causal = (ci <= ri) if direction == 0 else (ci >= ri)
    cum = jnp.dot(jnp.where(causal, 1.0, 0.0).astype(F32), da4,
                  preferred_element_type=F32, precision=HIGHEST)
    cum_t = cum.T
    dt_t = dt4.T
    end_row = q - 1 if direction == 0 else 0

    bm = b_ref[...].astype(BF16)
    cm = c_ref[...].astype(BF16)
    cb = _nt_dot(cm, bm)
    for r in range(SSM_HPG):
        sl = slice(r * SSM_HEAD_DIM, (r + 1) * SSM_HEAD_DIM)
        cc = cum[:, r:r + 1]
        seg = cc - cum_t[r:r + 1, :]
        decay = jnp.where(causal, jnp.exp(jnp.where(causal, seg, 0.0)), 0.0)
        w_ts = cb * decay * dt_t[r:r + 1, :]
        x_r = x_ref[:, sl]
        h_r = h_scr[r]
        y_diag = jnp.dot(w_ts.astype(BF16), x_r.astype(BF16), preferred_element_type=F32)
        y_off = _nt_dot(cm, h_r.astype(BF16)) * jnp.exp(cc)
        y_ref[:, sl] = y_diag + y_off
        end = cum[end_row:end_row + 1, r:r + 1]
        x_sc = x_r * (dt4[:, r:r + 1] * jnp.exp(end - cc))
        h_scr[r] = jnp.exp(end) * h_r + _tn_dot(x_sc.astype(BF16), bm)

    if want_state:
        @pl.when(c == nc - 1)
        def _():
            hl_ref[...] = h_scr[...]


def _ssd_scan(xbc, proj, dt_bias_row, a_log_row, state0, batch, seq, *, direction, want_state):
    nc = seq // SSM_CHUNK
    q = SSM_CHUNK
    zero_init = state0 is None
    xw = SSM_HPG * SSM_HEAD_DIM
    b_col0 = SSM_D_INNER // SSM_STATE
    c_col0 = b_col0 + SSM_GROUPS
    dt_col = (SSM_D_INNER + SSM_CONV_DIM) // SSM_DT_PAD

    def chunk(c):
        return c if direction == 0 else nc - 1 - c

    in_specs = [pl.BlockSpec((q, xw), lambda b, g, c: (b * nc + chunk(c), g)),
                pl.BlockSpec((q, SSM_STATE), lambda b, g, c: (b * nc + chunk(c), b_col0 + g)),
                pl.BlockSpec((q, SSM_STATE), lambda b, g, c: (b * nc + chunk(c), c_col0 + g)),
                pl.BlockSpec((q, SSM_DT_PAD), lambda b, g, c: (b * nc + chunk(c), dt_col)),
                pl.BlockSpec((1, SSM_DT_PAD), lambda b, g, c: (0, 0)),
                pl.BlockSpec((1, SSM_DT_PAD), lambda b, g, c: (0, 0))]
    args = [xbc, xbc, xbc, proj, dt_bias_row, a_log_row]
    if not zero_init:
        in_specs.append(pl.BlockSpec((None, None, SSM_HPG, SSM_HEAD_DIM, SSM_STATE),
                                     lambda b, g, c: (b, direction, g, 0, 0)))
        args.append(state0)
    out_specs = [pl.BlockSpec((q, xw), lambda b, g, c: (b * nc + chunk(c), g))]
    out_shape = [jax.ShapeDtypeStruct((batch * seq, SSM_D_INNER), F32)]
    if want_state:
        out_specs.append(pl.BlockSpec((None, SSM_HPG, SSM_HEAD_DIM, SSM_STATE), lambda b, g, c: (b, g, 0, 0)))
        out_shape.append(jax.ShapeDtypeStruct((batch, SSM_HEADS, SSM_HEAD_DIM, SSM_STATE), F32))
    return pl.pallas_call(
        functools.partial(_ssd_scan_kernel, direction=direction, zero_init=zero_init, want_state=want_state),
        grid=(batch, SSM_GROUPS, nc),
        in_specs=in_specs,
        out_specs=out_specs,
        out_shape=out_shape,
        scratch_shapes=[pltpu.VMEM((SSM_HPG, SSM_HEAD_DIM, SSM_STATE), F32)],
        compiler_params=_params(("arbitrary", "arbitrary", "arbitrary")),
        name="ssd_scan_fwd" if direction == 0 else "ssd_scan_bwd",
    )(*args)


def _ssd_out_kernel(yf_ref, yb_ref, xs_ref, z_ref, d_ref, nw_ref, w_ref, x_ref, mod_ref, o_ref, y_scr,
                    *, gate_idx):
    y = yf_ref[...] + yb_ref[...] + d_ref[...] * xs_ref[...]
    y = y * _silu(z_ref[...])
    gw = SSM_D_INNER // SSM_GROUPS
    for gi in range(SSM_GROUPS):
        sl = slice(gi * gw, (gi + 1) * gw)
        yg = y[:, sl]
        ms = jnp.mean(yg * yg, axis=-1, keepdims=True)
        y_scr[:, sl] = (yg * lax.rsqrt(ms + RMS_EPS) * nw_ref[:, sl]).astype(BF16)
    acc = jnp.dot(y_scr[...], w_ref[...], preferred_element_type=F32)
    o_ref[...] = x_ref[...] + mod_ref[0][gate_idx:gate_idx + 1] * acc


def _ssd_output(y_f, y_b, xbc, proj, d_row, norm_w, w_out_bf16, x, mods, *, gate_idx, cond_of_block, tb):
    t, d = x.shape
    di = SSM_D_INNER
    return pl.pallas_call(
        functools.partial(_ssd_out_kernel, gate_idx=gate_idx),
        grid=(t // tb,),
        in_specs=[pl.BlockSpec((tb, di), lambda i: (i, 0)),
                  pl.BlockSpec((tb, di), lambda i: (i, 0)),
                  pl.BlockSpec((tb, di), lambda i: (i, 0)),
                  pl.BlockSpec((tb, di), lambda i: (i, 0)),
                  pl.BlockSpec((1, di), lambda i: (0, 0)),
                  pl.BlockSpec((1, di), lambda i: (0, 0)),
                  pl.BlockSpec((di, d), lambda i: (0, 0)),
                  pl.BlockSpec((tb, d), lambda i: (i, 0)),
                  pl.BlockSpec((1, 6, d), lambda i: (cond_of_block(i, tb), 0, 0))],
        out_specs=pl.BlockSpec((tb, d), lambda i: (i, 0)),
        out_shape=jax.ShapeDtypeStruct((t, d), F32),
        scratch_shapes=[pltpu.VMEM((tb, di), BF16)],
        compiler_params=_params(("arbitrary",)),
        name="ssd_gate_norm_out",
    )(y_f, y_b, xbc, proj, d_row, norm_w.reshape(1, di), w_out_bf16, x, mods)


def _top16_rows(s):
    rows = []
    work = s
    for _ in range(PEER_TOPK):
        m = jnp.max(work, axis=0, keepdims=True)
        rows.append(m)
        work = jnp.where(work == m, -jnp.inf, work)
    return rows


def _stack_rows(rows, n):
    t = rows[0].shape[1]
    ri = lax.broadcasted_iota(jnp.int32, (n, t), 0)
    out = jnp.zeros((n, t), F32)
    for j, row in enumerate(rows):
        out = jnp.where(ri == j, row, out)
    return out


def _peer_topk_kernel(q_ref, keys_ref, r2_ref, e2_ref, n_ref, c_ref):
    k = PEER_TOPK
    tb = q_ref.shape[0]
    ri = lax.broadcasted_iota(jnp.int32, (k, tb), 0)

    s1 = _nt_dot(keys_ref[0], q_ref[:, :PEER_HALF])
    s2 = _nt_dot(keys_ref[1], q_ref[:, PEER_HALF:])
    v1 = _top16_rows(s1)
    v2 = _top16_rows(s2)
    v1a = _stack_rows(v1, k)
    v2a = _stack_rows(v2, k)
    blocks = []
    for i in range(4):
        blocks.append(jnp.where(ri < k // (i + 1), v1[i] + v2a, -jnp.inf))
    for j in range(4):
        blocks.append(jnp.where((ri >= 4) & (ri < k // (j + 1)), v1a + v2[j], -jnp.inf))
    cand = jnp.concatenate(blocks, axis=0)
    work = cand
    tau = None
    for _ in range(k):
        tau = jnp.max(work, axis=0, keepdims=True)
        work = jnp.where(work == tau, -jnp.inf, work)
    top = v1[0] + v2[0]
    chosen = cand >= tau
    z = jnp.sum(jnp.where(chosen, jnp.exp(jnp.where(chosen, cand, top) - top), 0.0),
                axis=0, keepdims=True)
    cnt = jnp.where(chosen, 1.0, 0.0)
    n16 = jnp.zeros((k, tb), F32)
    for i in range(4):
        n_i = jnp.sum(cnt[i * k:(i + 1) * k], axis=0, keepdims=True)
        n16 = jnp.where(ri == i, n_i, n16)
    for j in range(4):
        n16 = n16 + cnt[(4 + j) * k:(5 + j) * k]
    c16 = jnp.exp(v1a - v1[0]) / z
    n_a = jnp.zeros_like(s1)
    c_a = jnp.zeros_like(s1)
    for i in range(k):
        hit = s1 == v1[i]
        n_a = jnp.where(hit, n16[i:i + 1], n_a)
        c_a = jnp.where(hit, c16[i:i + 1], c_a)
    rank2 = jnp.zeros_like(s2)
    for j in range(k):
        rank2 = rank2 + jnp.where(v2[j] > s2, 1.0, 0.0)
    r2_ref[...] = rank2
    e2_ref[...] = jnp.exp(s2 - v2[0])
    n_ref[...] = n_a
    c_ref[...] = c_a


def _peer_topk(q_bf16, keys_bf16, tb):
    t = q_bf16.shape[0]
    shp = jax.ShapeDtypeStruct((PEER_HEADS, PEER_N_KEYS, t), F32)
    spec = pl.BlockSpec((None, PEER_N_KEYS, tb), lambda i, h: (h, 0, i))
    return pl.pallas_call(
        _peer_topk_kernel,
        grid=(t // tb, PEER_HEADS),
        in_specs=[pl.BlockSpec((tb, PEER_QUERY_DIM), lambda i, h: (i, h)),
                  pl.BlockSpec((None, 2, PEER_N_KEYS, PEER_HALF), lambda i, h: (h, 0, 0, 0))],
        out_specs=[spec, spec, spec, spec],
        out_shape=[shp, shp, shp, shp],
        compiler_params=_params(("arbitrary", "arbitrary")),
        name="peer_product_key_topk",
    )(q_bf16, keys_bf16)


def _peer_dense_kernel(f_ref, u_ref, vt_ref, r2_ref, e2_ref, n_ref, c_ref, x_ref, mod_ref, o_ref,
                       acc_ref, p_ref, *, gate_idx, eb):
    e = pl.program_id(1)
    ne = pl.num_programs(1)
    sub = eb // PEER_N_KEYS

    @pl.when(e == 0)
    def _():
        acc_ref[...] = jnp.zeros_like(acc_ref)

    act = _gelu_tanh(_nt_dot(u_ref[...], f_ref[...]))
    for s in range(sub):
        a = e * sub + s
        w = None
        for h in range(PEER_HEADS):
            n_row = n_ref[h, pl.ds(a, 1), :]
            c_row = c_ref[h, pl.ds(a, 1), :]
            term = jnp.where(r2_ref[h] < n_row, e2_ref[h] * c_row, 0.0)
            w = term if w is None else w + term
        rs = slice(s * PEER_N_KEYS, (s + 1) * PEER_N_KEYS)
        p_ref[rs, :] = (act[rs] * w).astype(BF16)
    acc_ref[...] += jnp.dot(vt_ref[...], p_ref[...], preferred_element_type=F32)

    @pl.when(e == ne - 1)
    def _():
        o_ref[...] = x_ref[...] + mod_ref[0][gate_idx:gate_idx + 1] * acc_ref[...].T


def _peer_dense(f_bf16, u_bf16, vt_bf16, r2, e2, n_a, c_a, x, mods, *, gate_idx, cond_of_block, tb, eb):
    t, d = x.shape
    ne = PEER_EXPERTS // eb
    aux = pl.BlockSpec((PEER_HEADS, PEER_N_KEYS, tb), lambda i, e: (0, 0, i))
    return pl.pallas_call(
        functools.partial(_peer_dense_kernel, gate_idx=gate_idx, eb=eb),
        grid=(t // tb, ne),
        in_specs=[pl.BlockSpec((tb, d), lambda i, e: (i, 0)),
                  pl.BlockSpec((eb, d), lambda i, e: (e, 0)),
                  pl.BlockSpec((d, eb), lambda i, e: (0, e)),
                  aux, aux, aux, aux,
                  pl.BlockSpec((tb, d), lambda i, e: (i, 0)),
                  pl.BlockSpec((1, 6, d), lambda i, e: (cond_of_block(i, tb), 0, 0))],
        out_specs=pl.BlockSpec((tb, d), lambda i, e: (i, 0)),
        out_shape=jax.ShapeDtypeStruct((t, d), F32),
        scratch_shapes=[pltpu.VMEM((d, tb), F32), pltpu.VMEM((eb, tb), BF16)],
        compiler_params=_params(("arbitrary", "arbitrary")),
        name="peer_dense_experts",
    )(f_bf16, u_bf16, vt_bf16, r2, e2, n_a, c_a, x, mods)


def _peer_block(x, norm_w, mods, wq_bf16, keys_bf16, u_bf16, vt_bf16, cond_of_block):
    q, f = _norm_mod_matmul(x, norm_w, mods, wq_bf16, shift_idx=3, cond_of_block=cond_of_block,
                            tb=512, tn=1024, out_dtypes=[BF16], emit_h=True, name="peer_query")
    r2, e2, n_a, c_a = _peer_topk(q, keys_bf16, tb=256)
    return _peer_dense(f, u_bf16, vt_bf16, r2, e2, n_a, c_a, x, mods, gate_idx=5,
                       cond_of_block=cond_of_block, tb=512, eb=512)


def _rmsnorm_kernel(x_ref, w_ref, o_ref):
    x = x_ref[...]
    ms = jnp.mean(x * x, axis=-1, keepdims=True)
    o_ref[...] = x * lax.rsqrt(ms + RMS_EPS) * w_ref[...]


def _final_rmsnorm(x, w, tb=512):
    t, d = x.shape
    return pl.pallas_call(
        _rmsnorm_kernel,
        grid=(t // tb,),
        in_specs=[pl.BlockSpec((tb, d), lambda i: (i, 0)), pl.BlockSpec((1, d), lambda i: (0, 0))],
        out_specs=pl.BlockSpec((tb, d), lambda i: (i, 0)),
        out_shape=jax.ShapeDtypeStruct((t, d), F32),
        compiler_params=_params(("arbitrary",)),
        name="final_rmsnorm",
    )(x, w.reshape(1, d))


def _prompt_cond(i, tb):
    return 0


def _make_sample_cond(seq):
    def cond(i, tb):
        return 1 + (i * tb) // seq
    return cond


def kernel(x_prompt, x_sample, cache_k, cache_v, state_ssm, c, c_ctx, ada_w, ada_b, norm1_w, norm2_w,
           final_norm_w, na_w_qkv, na_w_o, na_rpb, ssm_w_in, ssm_conv_w, ssm_conv_b, ssm_dt_bias,
           ssm_a_log, ssm_d, ssm_norm_w, ssm_w_out, peer_w_q, peer_keys, peer_u, peer_v):
    bp, lp, d = x_prompt.shape
    bs, ls, _ = x_sample.shape
    sample_cond = _make_sample_cond(ls)
    streams = [(x_prompt.reshape(bp * lp, d), _prompt_cond), (x_sample.reshape(bs * ls, d), sample_cond)]

    cond = jnp.zeros((8, d), F32).at[0].set(c_ctx).at[1:1 + bs].set(c)
    mods_all = _modulation(cond, ada_w, ada_b).reshape(DEPTH, 8, 6, d)

    new_k = new_v = new_state = None
    for i in range(DEPTH):
        j = i // 2
        mods = mods_all[i]
        xp, xs = streams[0][0], streams[1][0]
        if i % 2 == 0:
            wqkv = na_w_qkv[j].astype(BF16)
            wo = na_w_o[j].astype(BF16)
            qkv_p, qkv_p16 = _norm_mod_matmul(xp, norm1_w[i], mods, wqkv, shift_idx=0,
                                              cond_of_block=_prompt_cond, tb=512, tn=1024,
                                              out_dtypes=[F32, BF16], name="na_qkv_prompt")
            (qkv_s16,) = _norm_mod_matmul(xs, norm1_w[i], mods, wqkv, shift_idx=0,
                                          cond_of_block=sample_cond, tb=512, tn=1024,
                                          out_dtypes=[BF16], name="na_qkv_sample")
            new_k = qkv_p[:, d:2 * d].reshape(bp, 1, lp, NA_HEADS, NA_HEAD_DIM)
            new_v = qkv_p[:, 2 * d:].reshape(bp, 1, lp, NA_HEADS, NA_HEAD_DIM)
            o_p = _context_attention(qkv_p16, bp, lp)
            past = cache_k.shape[2]
            ck = cache_k[:, j].reshape(bs, past, d).astype(BF16)
            cv = cache_v[:, j].reshape(bs, past, d).astype(BF16)
            bias_tab = _na_bias_tables(na_rpb[j], ls // GRID_W)
            o_s = _na_latent_attention(qkv_s16, ck, cv, bias_tab, bs, ls)
            xp = _matmul_residual(o_p, wo, xp, mods, gate_idx=2, cond_of_block=_prompt_cond, tb=512,
                                  name="na_out_prompt")
            xs = _matmul_residual(o_s, wo, xs, mods, gate_idx=2, cond_of_block=sample_cond, tb=512,
                                  name="na_out_sample")
        else:
            w_in = ssm_w_in[j]
            n_main = SSM_D_INNER + SSM_CONV_DIM
            w_in_p = jnp.concatenate(
                [w_in, jnp.zeros((d, SSM_PROJ_DIM - w_in.shape[1]), w_in.dtype)], axis=1).astype(BF16)
            w_out = ssm_w_out[j].astype(BF16)
            pad = SSM_DT_PAD - 2 * SSM_HEADS
            dtb_row = jnp.pad(ssm_dt_bias[j].reshape(1, -1), ((0, 0), (0, pad)))
            alog_row = jnp.pad(ssm_a_log[j].reshape(1, -1), ((0, 0), (0, pad)))
            d_row = jnp.repeat(ssm_d[j], SSM_HEAD_DIM).reshape(1, SSM_D_INNER)
            outs = []
            for (x, cond_fn), (batch, seq, st0) in zip(
                    ((xp, _prompt_cond), (xs, sample_cond)),
                    ((bp, lp, None), (bs, ls, state_ssm[:, j]))):
                (proj,) = _norm_mod_matmul(x, norm1_w[i], mods, w_in_p, shift_idx=0, cond_of_block=cond_fn,
                                           tb=512, tn=896, out_dtypes=[F32], name="ssd_in_proj")
                xbc = _conv_silu(proj, ssm_conv_w[j], ssm_conv_b[j], batch, seq)
                want_state = st0 is None
                rf = _ssd_scan(xbc, proj, dtb_row, alog_row, st0, batch, seq, direction=0,
                               want_state=want_state)
                rb = _ssd_scan(xbc, proj, dtb_row, alog_row, st0, batch, seq, direction=1,
                               want_state=want_state)
                if want_state:
                    new_state = jnp.stack([rf[1], rb[1]], axis=1)[:, None]
                x_new = _ssd_output(rf[0], rb[0], xbc, proj, d_row, ssm_norm_w[j], w_out, x, mods,
                                    gate_idx=2, cond_of_block=cond_fn, tb=256)
                outs.append(x_new)
            xp, xs = outs
        wq = peer_w_q[i].astype(BF16)
        keys = peer_keys[i].astype(BF16)
        u = peer_u[i].astype(BF16)
        vt = peer_v[i].T.astype(BF16)
        xp = _peer_block(xp, norm2_w[i], mods, wq, keys, u, vt, _prompt_cond)
        xs = _peer_block(xs, norm2_w[i], mods, wq, keys, u, vt, sample_cond)
        streams = [(xp, _prompt_cond), (xs, sample_cond)]

    y_prompt = _final_rmsnorm(streams[0][0], final_norm_w).reshape(bp, lp, d)
    y_sample = _final_rmsnorm(streams[1][0], final_norm_w).reshape(bs, ls, d)
    return (y_prompt, y_sample, new_k, new_v, new_state)
```

```python
import functools
import math

import jax
import jax.numpy as jnp
from jax import lax
from jax.experimental import pallas as pl
from jax.experimental.pallas import tpu as pltpu

F32 = jnp.float32
BF16 = jnp.bfloat16
HIGHEST = lax.Precision.HIGHEST

D_MODEL = 1024
DEPTH = 2
GRID_W = 64
RMS_EPS = 1e-6
MASK_VALUE = -1e30
NA_HEADS = 16
NA_HEAD_DIM = 64
NA_WIN_H = 8
NA_WIN_W = 16
SSM_D_INNER = 2048
SSM_HEAD_DIM = 64
SSM_HEADS = 32
SSM_GROUPS = 8
SSM_HPG = 4
SSM_STATE = 128
SSM_CONV = 4
SSM_CHUNK = 128
SSM_CONV_DIM = 4096
SSM_DT_PAD = 128
SSM_PROJ_DIM = SSM_D_INNER + SSM_CONV_DIM + SSM_DT_PAD
PEER_HEADS = 8
PEER_N_KEYS = 128
PEER_EXPERTS = PEER_N_KEYS * PEER_N_KEYS
PEER_TOPK = 16
PEER_QUERY_DIM = 256
PEER_HALF = 128
BF16_ROWS = 16
GATE_LANES = 256

VMEM_LIMIT = 56 * 1024 * 1024


def _params(sem, vmem=VMEM_LIMIT):
    return pltpu.CompilerParams(dimension_semantics=sem, vmem_limit_bytes=vmem)


def _nt_dot(a, b):
    return lax.dot_general(a, b, (((1,), (1,)), ((), ())), preferred_element_type=F32)


def _tn_dot(a, b):
    return lax.dot_general(a, b, (((0,), (0,)), ((), ())), preferred_element_type=F32)


def _silu(x):
    return x / (1.0 + jnp.exp(-x))


def _softplus(x):
    return jnp.maximum(x, 0.0) + jnp.log1p(jnp.exp(-jnp.abs(x)))


def _gelu_tanh(x):
    c = math.sqrt(2.0 / math.pi)
    half = 0.5 * x
    return half + half * jnp.tanh(x * (c + (c * 0.044715) * (x * x)))


def _mod_kernel(cond_ref, w_ref, b_ref, o_ref):
    s = _silu(cond_ref[...])
    o_ref[0] = jnp.dot(s, w_ref[0], preferred_element_type=F32, precision=HIGHEST) + b_ref[0]


def _modulation(cond, ada_w, ada_b):
    depth, d, n = ada_w.shape
    tn = 1536
    return pl.pallas_call(
        _mod_kernel,
        grid=(depth, n // tn),
        in_specs=[pl.BlockSpec((8, d), lambda i, j: (0, 0)),
                  pl.BlockSpec((1, d, tn), lambda i, j: (i, 0, j)),
                  pl.BlockSpec((1, 1, tn), lambda i, j: (i, 0, j))],
        out_specs=pl.BlockSpec((1, 8, tn), lambda i, j: (i, 0, j)),
        out_shape=jax.ShapeDtypeStruct((depth, 8, n), F32),
        compiler_params=_params(("arbitrary", "arbitrary")),
        name="adaln_modulation",
    )(cond, ada_w, ada_b.reshape(depth, 1, n))


def _nmm_kernel(x_ref, nw_ref, mod_ref, w_ref, *refs, shift_idx, n_out):
    out_refs, h_scr = refs[:n_out], refs[n_out]
    j = pl.program_id(1)

    @pl.when(j == 0)
    def _():
        x = x_ref[...]
        ms = jnp.mean(x * x, axis=-1, keepdims=True)
        xn = x * lax.rsqrt(ms + RMS_EPS) * nw_ref[...]
        m = mod_ref[0]
        h = xn * (1.0 + m[shift_idx + 1:shift_idx + 2]) + m[shift_idx:shift_idx + 1]
        h_scr[...] = h.astype(BF16)

    acc = jnp.dot(h_scr[...], w_ref[...], preferred_element_type=F32)
    return acc, out_refs, h_scr, j


def _nmm_kernel_main(x_ref, nw_ref, mod_ref, w_ref, *refs, shift_idx, out_dtypes, emit_h):
    n_out = len(out_dtypes) + (1 if emit_h else 0)
    acc, out_refs, h_scr, j = _nmm_kernel(x_ref, nw_ref, mod_ref, w_ref, *refs,
                                          shift_idx=shift_idx, n_out=n_out)
    for r, dt in zip(out_refs, out_dtypes):
        r[...] = acc.astype(dt)
    if emit_h:
        @pl.when(j == 0)
        def _():
            out_refs[-1][...] = h_scr[...]


def _norm_mod_matmul(x, norm_w, mods, w_bf16, *, shift_idx, cond_of_block, tb, tn,
                     out_dtypes, emit_h=False, name="norm_mod_matmul"):
    t, d = x.shape
    n = w_bf16.shape[1]
    assert t % tb == 0 and n % tn == 0
    out_shape = [jax.ShapeDtypeStruct((t, n), dt) for dt in out_dtypes]
    out_specs = [pl.BlockSpec((tb, tn), lambda i, j: (i, j)) for _ in out_dtypes]
    if emit_h:
        out_shape.append(jax.ShapeDtypeStruct((t, d), BF16))
        out_specs.append(pl.BlockSpec((tb, d), lambda i, j: (i, 0)))
    kern = functools.partial(_nmm_kernel_main, shift_idx=shift_idx,
                             out_dtypes=tuple(out_dtypes), emit_h=emit_h)
    return pl.pallas_call(
        kern,
        grid=(t // tb, n // tn),
        in_specs=[pl.BlockSpec((tb, d), lambda i, j: (i, 0)),
                  pl.BlockSpec((1, d), lambda i, j: (0, 0)),
                  pl.BlockSpec((1, 6, d), lambda i, j: (cond_of_block(i, tb), 0, 0)),
                  pl.BlockSpec((d, tn), lambda i, j: (0, j))],
        out_specs=out_specs,
        out_shape=out_shape,
        scratch_shapes=[pltpu.VMEM((tb, d), BF16)],
        compiler_params=_params(("arbitrary", "arbitrary")),
        name=name,
    )(x, norm_w.reshape(1, d), mods, w_bf16)


def _mm_res_kernel(a_ref, w_ref, x_ref, mod_ref, o_ref, *, gate_idx):
    acc = jnp.dot(a_ref[...], w_ref[...], preferred_element_type=F32)
    o_ref[...] = x_ref[...] + mod_ref[0][gate_idx:gate_idx + 1] * acc


def _matmul_residual(a, w_bf16, x, mods, *, gate_idx, cond_of_block, tb, name="matmul_residual"):
    t, k = a.shape
    d = w_bf16.shape[1]
    return pl.pallas_call(
        functools.partial(_mm_res_kernel, gate_idx=gate_idx),
        grid=(t // tb,),
        in_specs=[pl.BlockSpec((tb, k), lambda i: (i, 0)),
                  pl.BlockSpec((k, d), lambda i: (0, 0)),
                  pl.BlockSpec((tb, d), lambda i: (i, 0)),
                  pl.BlockSpec((1, 6, d), lambda i: (cond_of_block(i, tb), 0, 0))],
        out_specs=pl.BlockSpec((tb, d), lambda i: (i, 0)),
        out_shape=jax.ShapeDtypeStruct((t, d), F32),
        compiler_params=_params(("arbitrary",)),
        name=name,
    )(a, w_bf16, x, mods)


def _ctx_attn_kernel(q_ref, k_ref, v_ref, o_ref):
    scale = NA_HEAD_DIM ** -0.5
    for h in range(NA_HEADS):
        sl = slice(h * NA_HEAD_DIM, (h + 1) * NA_HEAD_DIM)
        s = _nt_dot(q_ref[:, sl], k_ref[:, sl]) * scale
        m = jnp.max(s, axis=-1, keepdims=True)
        p = jnp.exp(s - m)
        l = jnp.sum(p, axis=-1, keepdims=True)
        o = jnp.dot(p.astype(BF16), v_ref[:, sl], preferred_element_type=F32) / l
        o_ref[:, sl] = o.astype(o_ref.dtype)


def _context_attention(qkv_bf16, batch, seq):
    d = D_MODEL
    return pl.pallas_call(
        _ctx_attn_kernel,
        grid=(batch,),
        in_specs=[pl.BlockSpec((seq, d), lambda b: (b, 0)),
                  pl.BlockSpec((seq, d), lambda b: (b, 1)),
                  pl.BlockSpec((seq, d), lambda b: (b, 2))],
        out_specs=pl.BlockSpec((seq, d), lambda b: (b, 0)),
        out_shape=jax.ShapeDtypeStruct((batch * seq, d), BF16),
        compiler_params=_params(("arbitrary",)),
        name="context_attention",
    )(qkv_bf16, qkv_bf16, qkv_bf16)


def _na_window_start(r, rows):
    return jnp.clip(r - NA_WIN_H // 2, 0, rows - NA_WIN_H)


def _na_kernel(q_ref, k_ref, v_ref, ck_ref, cv_ref, bias_ref, o_ref, *, rows):
    scale = NA_HEAD_DIM ** -0.5
    r = pl.program_id(1)
    n_loc = NA_WIN_H * GRID_W
    start = pl.multiple_of(_na_window_start(r, rows) * GRID_W, GRID_W)
    for h in range(NA_HEADS):
        sl = slice(h * NA_HEAD_DIM, (h + 1) * NA_HEAD_DIM)
        q = q_ref[:, sl]
        k_loc = k_ref[pl.ds(start, n_loc), sl]
        v_loc = v_ref[pl.ds(start, n_loc), sl]
        bias = bias_ref[h]
        s_loc = jnp.where(bias > 0.5 * MASK_VALUE, _nt_dot(q, k_loc) * scale + bias, MASK_VALUE)
        s_ctx = _nt_dot(q, ck_ref[:, sl]) * scale
        m = jnp.maximum(jnp.max(s_loc, axis=-1, keepdims=True),
                        jnp.max(s_ctx, axis=-1, keepdims=True))
        p_loc = jnp.exp(s_loc - m)
        p_ctx = jnp.exp(s_ctx - m)
        l = jnp.sum(p_loc, axis=-1, keepdims=True) + jnp.sum(p_ctx, axis=-1, keepdims=True)
        o = (jnp.dot(p_loc.astype(BF16), v_loc, preferred_element_type=F32)
             + jnp.dot(p_ctx.astype(BF16), cv_ref[:, sl], preferred_element_type=F32)) / l
        o_ref[:, sl] = o.astype(o_ref.dtype)


def _na_bias_tables(rpb, rows):
    cols = jnp.arange(GRID_W)
    col_start = jnp.clip(cols - NA_WIN_W // 2, 0, GRID_W - NA_WIN_W)
    col_ok = (cols[None, :] >= col_start[:, None]) & (cols[None, :] < col_start[:, None] + NA_WIN_W)
    dc = jnp.clip(cols[None, :] - cols[:, None], -(NA_WIN_W - 1), NA_WIN_W - 1) + (NA_WIN_W - 1)
    full = rpb[:, :, dc].astype(F32)
    full = jnp.where(col_ok[None, None], full, MASK_VALUE)
    variants = []
    for v in range(NA_WIN_H):
        t = full[:, v:v + NA_WIN_H]
        variants.append(jnp.transpose(t, (0, 2, 1, 3)).reshape(NA_HEADS, GRID_W, NA_WIN_H * GRID_W))
    return jnp.stack(variants, axis=0)


def _na_latent_attention(qkv_bf16, ck_bf16, cv_bf16, bias_tab, batch, n_tok):
    d = D_MODEL
    rows = n_tok // GRID_W
    past = ck_bf16.shape[1]
    n_loc = NA_WIN_H * GRID_W

    def variant(b, r):
        return _na_window_start(r, rows) - r + (NA_WIN_H - 1)

    return pl.pallas_call(
        functools.partial(_na_kernel, rows=rows),
        grid=(batch, rows),
        in_specs=[pl.BlockSpec((GRID_W, d), lambda b, r: (b * rows + r, 0)),
                  pl.BlockSpec((n_tok, d), lambda b, r: (b, 1)),
                  pl.BlockSpec((n_tok, d), lambda b, r: (b, 2)),
                  pl.BlockSpec((None, past, d), lambda b, r: (b, 0, 0)),
                  pl.BlockSpec((None, past, d), lambda b, r: (b, 0, 0)),
                  pl.BlockSpec((None, NA_HEADS, GRID_W, n_loc), lambda b, r: (variant(b, r), 0, 0, 0))],
        out_specs=pl.BlockSpec((GRID_W, d), lambda b, r: (b * rows + r, 0)),
        out_shape=jax.ShapeDtypeStruct((batch * n_tok, d), BF16),
        compiler_params=_params(("arbitrary", "arbitrary")),
        name="neighbourhood_attention",
    )(qkv_bf16, qkv_bf16, qkv_bf16, ck_bf16, cv_bf16, bias_tab)


def _conv_silu_kernel(x_ref, w_ref, b_ref, o_ref):
    x = x_ref[...]
    n = x.shape[0]
    row = lax.broadcasted_iota(jnp.int32, x.shape, 0)
    w = w_ref[...]
    xm1 = jnp.where(row >= 1, pltpu.roll(x, 1, 0), 0.0)
    xp1 = jnp.where(row < n - 1, pltpu.roll(x, n - 1, 0), 0.0)
    xp2 = jnp.where(row < n - 2, pltpu.roll(x, n - 2, 0), 0.0)
    y = w[0:1] * xm1 + w[1:2] * x + w[2:3] * xp1 + w[3:4] * xp2 + b_ref[...]
    o_ref[...] = _silu(y)


def _conv_silu(proj, conv_w, conv_b, batch, seq):
    cb = 512
    c0 = SSM_D_INNER // cb
    return pl.pallas_call(
        _conv_silu_kernel,
        grid=(batch, SSM_CONV_DIM // cb),
        in_specs=[pl.BlockSpec((seq, cb), lambda b, j: (b, c0 + j)),
                  pl.BlockSpec((SSM_CONV, cb), lambda b, j: (0, j)),
                  pl.BlockSpec((1, cb), lambda b, j: (0, j))],
        out_specs=pl.BlockSpec((seq, cb), lambda b, j: (b, j)),
        out_shape=jax.ShapeDtypeStruct((batch * seq, SSM_CONV_DIM), F32),
        compiler_params=_params(("arbitrary", "arbitrary")),
        name="ssd_conv_silu",
    )(proj, conv_w, conv_b.reshape(1, SSM_CONV_DIM))


def _ssd_scan_kernel(*refs, direction, zero_init, want_state):
    it = iter(refs)
    x_ref, b_ref, c_ref, dt_ref, dtb_ref, alog_ref = (next(it) for _ in range(6))
    h0_ref = None if zero_init else next(it)
    y_ref = next(it)
    hl_ref = next(it) if want_state else None
    h_scr = next(it)

    q = SSM_CHUNK
    c = pl.program_id(1)
    nc = pl.num_programs(1)

    @pl.when(c == 0)
    def _():
        if zero_init:
            h_scr[...] = jnp.zeros_like(h_scr)
        else:
            h_scr[...] = h0_ref[...]

    dt_all = _softplus(dt_ref[...] + dtb_ref[...])
    da_all = dt_all * (-jnp.exp(alog_ref[...]))
    ri = lax.broadcasted_iota(jnp.int32, (q, q), 0)
    ci = lax.broadcasted_iota(jnp.int32, (q, q), 1)
    causal = (ci <= ri) if direction == 0 else (ci >= ri)
    cum = jnp.dot(jnp.where(causal, 1.0, 0.0).astype(F32), da_all,
                  preferred_element_type=F32, precision=HIGHEST)
    cum_t = cum.T
    dt_t = dt_all.T
    end_row = q - 1 if direction == 0 else 0
    e_end = jnp.exp(cum[end_row:end_row + 1, :])
    dt_end_t = dt_t * jnp.exp(cum_t[:, end_row:end_row + 1] - cum_t)

    for g in range(SSM_GROUPS):
        gs = slice(g * SSM_STATE, (g + 1) * SSM_STATE)
        bm = b_ref[:, gs].astype(BF16)
        cm32 = c_ref[:, gs]
        cb = _nt_dot(cm32.astype(BF16), bm)
        x_t = [x_ref[:, (g * SSM_HPG + 2 * k) * SSM_HEAD_DIM:(g * SSM_HPG + 2 * k + 2) * SSM_HEAD_DIM].T
               for k in range(SSM_HPG // 2)]
        for r in range(SSM_HPG):
            hd = g * SSM_HPG + r
            j = direction * SSM_HEADS + hd
            sl = slice(hd * SSM_HEAD_DIM, (hd + 1) * SSM_HEAD_DIM)
            cum_b = jnp.broadcast_to(cum[:, j:j + 1], (q, q))
            seg = cum_b - cum_t[j:j + 1, :]
            decay = jnp.where(causal, jnp.exp(jnp.where(causal, seg, 0.0)), 0.0)
            w_ts = cb * decay * dt_t[j:j + 1, :]
            x_r = x_ref[:, sl]
            h_r = h_scr[hd]
            c_dec = (cm32 * jnp.exp(cum_b)).astype(BF16)
            y_ref[:, sl] = (jnp.dot(w_ts.astype(BF16), x_r.astype(BF16), preferred_element_type=F32)
                            + _nt_dot(c_dec, h_r.astype(BF16)))
            x_r_t = x_t[r // 2][(r % 2) * SSM_HEAD_DIM:(r % 2 + 1) * SSM_HEAD_DIM]
            x_sc_t = x_r_t * dt_end_t[j:j + 1, :]
            h_scr[hd] = e_end[:, j:j + 1] * h_r + jnp.dot(x_sc_t.astype(BF16), bm, preferred_element_type=F32)

    if want_state:
        @pl.when(c == nc - 1)
        def _():
            hl_ref[...] = h_scr[...]


def _ssd_scan(xbc, proj, dt_bias_row, a_log_row, state0, batch, seq, *, direction, want_state):
    nc = seq // SSM_CHUNK
    q = SSM_CHUNK
    zero_init = state0 is None
    gn = SSM_GROUPS * SSM_STATE
    b_col = SSM_D_INNER // gn
    dt_col = (SSM_D_INNER + SSM_CONV_DIM) // SSM_DT_PAD

    def chunk(c):
        return c if direction == 0 else nc - 1 - c

    in_specs = [pl.BlockSpec((q, SSM_D_INNER), lambda b, c: (b * nc + chunk(c), 0)),
                pl.BlockSpec((q, gn), lambda b, c: (b * nc + chunk(c), b_col)),
                pl.BlockSpec((q, gn), lambda b, c: (b * nc + chunk(c), b_col + 1)),
                pl.BlockSpec((q, SSM_DT_PAD), lambda b, c: (b * nc + chunk(c), dt_col)),
                pl.BlockSpec((1, SSM_DT_PAD), lambda b, c: (0, 0)),
                pl.BlockSpec((1, SSM_DT_PAD), lambda b, c: (0, 0))]
    args = [xbc, xbc, xbc, proj, dt_bias_row, a_log_row]
    if not zero_init:
        in_specs.append(pl.BlockSpec((None, None, SSM_HEADS, SSM_HEAD_DIM, SSM_STATE),
                                     lambda b, c: (b, direction, 0, 0, 0)))
        args.append(state0)
    out_specs = [pl.BlockSpec((q, SSM_D_INNER), lambda b, c: (b * nc + chunk(c), 0))]
    out_shape = [jax.ShapeDtypeStruct((batch * seq, SSM_D_INNER), F32)]
    if want_state:
        out_specs.append(pl.BlockSpec((None, SSM_HEADS, SSM_HEAD_DIM, SSM_STATE), lambda b, c: (b, 0, 0, 0)))
        out_shape.append(jax.ShapeDtypeStruct((batch, SSM_HEADS, SSM_HEAD_DIM, SSM_STATE), F32))
    return pl.pallas_call(
        functools.partial(_ssd_scan_kernel, direction=direction, zero_init=zero_init, want_state=want_state),
        grid=(batch, nc),
        in_specs=in_specs,
        out_specs=out_specs,
        out_shape=out_shape,
        scratch_shapes=[pltpu.VMEM((SSM_HEADS, SSM_HEAD_DIM, SSM_STATE), F32)],
        compiler_params=_params(("arbitrary", "arbitrary")),
        name="ssd_scan_fwd" if direction == 0 else "ssd_scan_bwd",
    )(*args)


def _ssd_out_kernel(yf_ref, yb_ref, xs_ref, z_ref, d_ref, nw_ref, w_ref, x_ref, mod_ref, o_ref, y_scr,
                    *, gate_idx):
    y = yf_ref[...] + yb_ref[...] + d_ref[...] * xs_ref[...]
    y = y * _silu(z_ref[...])
    gw = SSM_D_INNER // SSM_GROUPS
    for gi in range(SSM_GROUPS):
        sl = slice(gi * gw, (gi + 1) * gw)
        yg = y[:, sl]
        ms = jnp.mean(yg * yg, axis=-1, keepdims=True)
        y_scr[:, sl] = (yg * lax.rsqrt(ms + RMS_EPS) * nw_ref[:, sl]).astype(BF16)
    acc = jnp.dot(y_scr[...], w_ref[...], preferred_element_type=F32)
    o_ref[...] = x_ref[...] + mod_ref[0][gate_idx:gate_idx + 1] * acc


def _ssd_output(y_f, y_b, xbc, proj, d_row, norm_w, w_out_bf16, x, mods, *, gate_idx, cond_of_block, tb):
    t, d = x.shape
    di = SSM_D_INNER
    return pl.pallas_call(
        functools.partial(_ssd_out_kernel, gate_idx=gate_idx),
        grid=(t // tb,),
        in_specs=[pl.BlockSpec((tb, di), lambda i: (i, 0)),
                  pl.BlockSpec((tb, di), lambda i: (i, 0)),
                  pl.BlockSpec((tb, di), lambda i: (i, 0)),
                  pl.BlockSpec((tb, di), lambda i: (i, 0)),
                  pl.BlockSpec((1, di), lambda i: (0, 0)),
                  pl.BlockSpec((1, di), lambda i: (0, 0)),
                  pl.BlockSpec((di, d), lambda i: (0, 0)),
                  pl.BlockSpec((tb, d), lambda i: (i, 0)),
                  pl.BlockSpec((1, 6, d), lambda i: (cond_of_block(i, tb), 0, 0))],
        out_specs=pl.BlockSpec((tb, d), lambda i: (i, 0)),
        out_shape=jax.ShapeDtypeStruct((t, d), F32),
        scratch_shapes=[pltpu.VMEM((tb, di), BF16)],
        compiler_params=_params(("arbitrary",)),
        name="ssd_gate_norm_out",
    )(y_f, y_b, xbc, proj, d_row, norm_w.reshape(1, di), w_out_bf16, x, mods)


def _top16_rows(s):
    rows = []
    work = s
    for _ in range(PEER_TOPK):
        m = jnp.max(work, axis=0, keepdims=True)
        rows.append(m)
        work = jnp.where(work == m, -jnp.inf, work)
    return rows


def _stack_rows(rows, n):
    t = rows[0].shape[1]
    ri = lax.broadcasted_iota(jnp.int32, (n, t), 0)
    out = jnp.zeros((n, t), F32)
    for j, row in enumerate(rows):
        out = jnp.where(ri == j, row, out)
    return out


def _peer_topk_kernel(q_ref, keys_ref, r2_ref, e2_ref, n_ref, c_ref):
    k = PEER_TOPK
    tb = q_ref.shape[0]
    ri = lax.broadcasted_iota(jnp.int32, (k, tb), 0)

    s1 = _nt_dot(keys_ref[0], q_ref[:, :PEER_HALF])
    s2 = _nt_dot(keys_ref[1], q_ref[:, PEER_HALF:])
    v1 = _top16_rows(s1)
    v2 = _top16_rows(s2)
    v1a = _stack_rows(v1, k)
    v2a = _stack_rows(v2, k)
    r8 = lax.broadcasted_iota(jnp.int32, (8, tb), 0)
    v1lo, v1hi, v2lo = v1a[0:8], v1a[8:16], v2a[0:8]
    ninf = -jnp.inf
    blocks = [v1[0] + v2a,
              v1[1] + v2lo,
              jnp.where(r8 < 5, v1[2] + v2lo, ninf),
              jnp.where(r8 < 4, v1[3] + v2lo, ninf),
              jnp.where(r8 >= 4, v1lo + v2[0], ninf),
              v1hi + v2[0],
              jnp.where(r8 >= 4, v1lo + v2[1], ninf),
              jnp.where(r8 == 4, v1lo + v2[2], ninf)]
    cand = jnp.concatenate(blocks, axis=0)
    work = cand
    tau = None
    for _ in range(k):
        tau = jnp.max(work, axis=0, keepdims=True)
        work = jnp.where(work == tau, -jnp.inf, work)
    top = v1[0] + v2[0]
    chosen = cand >= tau
    z = jnp.sum(jnp.where(chosen, jnp.exp(jnp.where(chosen, cand, top) - top), 0.0),
                axis=0, keepdims=True)
    cnt = jnp.where(chosen, 1.0, 0.0)
    n_lo = cnt[40:48] + cnt[56:64] + cnt[64:72]
    for i, rows in enumerate((slice(0, 16), slice(16, 24), slice(24, 32), slice(32, 40))):
        n_lo = jnp.where(r8 == i, jnp.sum(cnt[rows], axis=0, keepdims=True), n_lo)
    n16 = jnp.concatenate([n_lo, cnt[48:56]], axis=0)
    c16 = jnp.exp(v1a - v1[0]) / z
    n_a = jnp.zeros_like(s1)
    c_a = jnp.zeros_like(s1)
    for i in range(k):
        hit = s1 == v1[i]
        n_a = jnp.where(hit, n16[i:i + 1], n_a)
        c_a = jnp.where(hit, c16[i:i + 1], c_a)
    rank2 = jnp.zeros_like(s2)
    for j in range(k):
        rank2 = rank2 + jnp.where(v2[j] > s2, 1.0, 0.0)
    r2_ref[...] = rank2.astype(BF16)
    e2_ref[...] = jnp.exp(s2 - v2[0]).astype(BF16)
    n_ref[...] = n_a
    c_ref[...] = c_a


def _peer_topk(q_bf16, keys_bf16, tb):
    t = q_bf16.shape[0]
    shp = jax.ShapeDtypeStruct((PEER_HEADS, PEER_N_KEYS, t), F32)
    shp16 = jax.ShapeDtypeStruct((PEER_HEADS, PEER_N_KEYS, t), BF16)
    spec = pl.BlockSpec((None, PEER_N_KEYS, tb), lambda i, h: (h, 0, i))
    return pl.pallas_call(
        _peer_topk_kernel,
        grid=(t // tb, PEER_HEADS),
        in_specs=[pl.BlockSpec((tb, PEER_QUERY_DIM), lambda i, h: (i, h)),
                  pl.BlockSpec((None, 2, PEER_N_KEYS, PEER_HALF), lambda i, h: (h, 0, 0, 0))],
        out_specs=[spec, spec, spec, spec],
        out_shape=[shp16, shp16, shp, shp],
        compiler_params=_params(("arbitrary", "arbitrary")),
        name="peer_product_key_topk",
    )(q_bf16, keys_bf16)


def _peer_gate_and_project(a0, ht_ref, p_ref, vt_ref, r2_ref, e2_ref, n_ref, c_ref, acc_ref, sub, tb):
    n_bt = PEER_N_KEYS // BF16_ROWS
    for s in range(sub):
        a = a0 + s
        for l in range(tb // GATE_LANES):
            ls = slice(l * GATE_LANES, (l + 1) * GATE_LANES)
            w = [None] * n_bt
            for h in range(PEER_HEADS):
                n_b = jnp.broadcast_to(n_ref[h, pl.ds(a, 1), ls], (BF16_ROWS, GATE_LANES)).astype(BF16)
                c_b = jnp.broadcast_to(c_ref[h, pl.ds(a, 1), ls], (BF16_ROWS, GATE_LANES)).astype(BF16)
                for i in range(n_bt):
                    bs = slice(i * BF16_ROWS, (i + 1) * BF16_ROWS)
                    term = jnp.where(r2_ref[h, bs, ls] < n_b, e2_ref[h, bs, ls] * c_b, jnp.zeros_like(c_b))
                    w[i] = term if w[i] is None else w[i] + term
            for i in range(n_bt):
                rs = slice(s * PEER_N_KEYS + i * BF16_ROWS, s * PEER_N_KEYS + (i + 1) * BF16_ROWS)
                p_ref[rs, ls] = _gelu_tanh(ht_ref[rs, ls]).astype(BF16) * w[i]
    acc_ref[...] += jnp.dot(vt_ref[...], p_ref[...], preferred_element_type=F32)


def _peer_dense_kernel(f_ref, u_ref, vt_ref, r2_ref, e2_ref, n_ref, c_ref, x_ref, mod_ref, o_ref,
                       acc_ref, p_ref, ht0_ref, ht1_ref, *, gate_idx, eb):
    e = pl.program_id(1)
    ne = pl.num_programs(1) - 1
    sub = eb // PEER_N_KEYS
    tb = f_ref.shape[0]

    @pl.when(e == 0)
    def _():
        acc_ref[...] = jnp.zeros_like(acc_ref)
        ht1_ref[...] = jnp.zeros_like(ht1_ref)

    a_prev = jnp.maximum(e - 1, 0) * sub
    for parity, (ht_new, ht_old) in enumerate(((ht0_ref, ht1_ref), (ht1_ref, ht0_ref))):
        @pl.when(e % 2 == parity)
        def _(ht_new=ht_new, ht_old=ht_old):
            ht_new[...] = _nt_dot(u_ref[...], f_ref[...])
            _peer_gate_and_project(a_prev, ht_old, p_ref, vt_ref, r2_ref, e2_ref, n_ref, c_ref, acc_ref, sub, tb)

    @pl.when(e == ne)
    def _():
        o_ref[...] = x_ref[...] + mod_ref[0][gate_idx:gate_idx + 1] * acc_ref[...].T


def _peer_dense(f_bf16, u_bf16, vt_bf16, r2, e2, n_a, c_a, x, mods, *, gate_idx, cond_of_block, tb, eb):
    t, d = x.shape
    assert t % tb == 0 and tb % GATE_LANES == 0 and PEER_EXPERTS % eb == 0 and eb % PEER_N_KEYS == 0
    ne = PEER_EXPERTS // eb
    aux =pl.BlockSpec((PEER_HEADS, PEER_N_KEYS, tb), lambda i, e: (0, 0, i))
    return pl.pallas_call(
        functools.partial(_peer_dense_kernel, gate_idx=gate_idx, eb=eb),
        grid=(t // tb, ne + 1),
        in_specs=[pl.BlockSpec((tb, d), lambda i, e: (i, 0)),
                  pl.BlockSpec((eb, d), lambda i, e: (jnp.minimum(e, ne - 1), 0)),
                  pl.BlockSpec((d, eb), lambda i, e: (0, jnp.maximum(e - 1, 0))),
                  aux, aux, aux, aux,
                  pl.BlockSpec((tb, d), lambda i, e: (i, 0)),
                  pl.BlockSpec((1, 6, d), lambda i, e: (cond_of_block(i, tb), 0, 0))],
        out_specs=pl.BlockSpec((tb, d), lambda i, e: (i, 0)),
        out_shape=jax.ShapeDtypeStruct((t, d), F32),
        scratch_shapes=[pltpu.VMEM((d, tb), F32), pltpu.VMEM((eb, tb), BF16),
                        pltpu.VMEM((eb, tb), F32), pltpu.VMEM((eb, tb), F32)],
        compiler_params=_params(("arbitrary", "arbitrary")),
        name="peer_dense_experts",
    )(f_bf16, u_bf16, vt_bf16, r2, e2, n_a, c_a, x, mods)


def _peer_block(x, norm_w, mods, wq_bf16, keys_bf16, u_bf16, vt_bf16, cond_of_block):
    q, f = _norm_mod_matmul(x, norm_w, mods, wq_bf16, shift_idx=3, cond_of_block=cond_of_block,
                            tb=512, tn=1024, out_dtypes=[BF16], emit_h=True, name="peer_query")
    r2, e2, n_a, c_a = _peer_topk(q, keys_bf16, tb=256)
    return _peer_dense(f, u_bf16, vt_bf16, r2, e2, n_a, c_a, x, mods, gate_idx=5,
                       cond_of_block=cond_of_block, tb=512, eb=512)


def _rmsnorm_kernel(x_ref, w_ref, o_ref):
    x = x_ref[...]
    ms = jnp.mean(x * x, axis=-1, keepdims=True)
    o_ref[...] = x * lax.rsqrt(ms + RMS_EPS) * w_ref[...]


def _final_rmsnorm(x, w, tb=512):
    t, d = x.shape
    return pl.pallas_call(
        _rmsnorm_kernel,
        grid=(t // tb,),
        in_specs=[pl.BlockSpec((tb, d), lambda i: (i, 0)), pl.BlockSpec((1, d), lambda i: (0, 0))],
        out_specs=pl.BlockSpec((tb, d), lambda i: (i, 0)),
        out_shape=jax.ShapeDtypeStruct((t, d), F32),
        compiler_params=_params(("arbitrary",)),
        name="final_rmsnorm",
    )(x, w.reshape(1, d))


def _prompt_cond(i, tb):
    return 0


def _make_sample_cond(seq):
    def cond(i, tb):
        return 1 + (i * tb) // seq
    return cond


def kernel(x_prompt, x_sample, cache_k, cache_v, state_ssm, c, c_ctx, ada_w, ada_b, norm1_w, norm2_w,
           final_norm_w, na_w_qkv, na_w_o, na_rpb, ssm_w_in, ssm_conv_w, ssm_conv_b, ssm_dt_bias,
           ssm_a_log, ssm_d, ssm_norm_w, ssm_w_out, peer_w_q, peer_keys, peer_u, peer_v):
    bp, lp, d = x_prompt.shape
    bs, ls, _ = x_sample.shape
    sample_cond = _make_sample_cond(ls)
    streams = [(x_prompt.reshape(bp * lp, d), _prompt_cond), (x_sample.reshape(bs * ls, d), sample_cond)]

    cond = jnp.zeros((8, d), F32).at[0].set(c_ctx).at[1:1 + bs].set(c)
    mods_all = _modulation(cond, ada_w, ada_b).reshape(DEPTH, 8, 6, d)

    new_k = new_v = new_state = None
    for i in range(DEPTH):
        j = i // 2
        mods = mods_all[i]
        xp, xs = streams[0][0], streams[1][0]
        if i % 2 == 0:
            wqkv = na_w_qkv[j].astype(BF16)
            wo = na_w_o[j].astype(BF16)
            qkv_p, qkv_p16 = _norm_mod_matmul(xp, norm1_w[i], mods, wqkv, shift_idx=0,
                                              cond_of_block=_prompt_cond, tb=512, tn=1024,
                                              out_dtypes=[F32, BF16], name="na_qkv_prompt")
            (qkv_s16,) = _norm_mod_matmul(xs, norm1_w[i], mods, wqkv, shift_idx=0,
                                          cond_of_block=sample_cond, tb=512, tn=1024,
                                          out_dtypes=[BF16], name="na_qkv_sample")
            new_k = qkv_p[:, d:2 * d].reshape(bp, 1, lp, NA_HEADS, NA_HEAD_DIM)
            new_v = qkv_p[:, 2 * d:].reshape(bp, 1, lp, NA_HEADS, NA_HEAD_DIM)
            o_p = _context_attention(qkv_p16, bp, lp)
            past = cache_k.shape[2]
            ck = cache_k[:, j].reshape(bs, past, d).astype(BF16)
            cv = cache_v[:, j].reshape(bs, past, d).astype(BF16)
            bias_tab = _na_bias_tables(na_rpb[j], ls // GRID_W)
            o_s = _na_latent_attention(qkv_s16, ck, cv, bias_tab, bs, ls)
            xp = _matmul_residual(o_p, wo, xp, mods, gate_idx=2, cond_of_block=_prompt_cond, tb=512,
                                  name="na_out_prompt")
            xs = _matmul_residual(o_s, wo, xs, mods, gate_idx=2, cond_of_block=sample_cond, tb=512,
                                  name="na_out_sample")
        else:
            w_in = ssm_w_in[j]
            n_main = SSM_D_INNER + SSM_CONV_DIM
            w_in_p = jnp.concatenate(
                [w_in, jnp.zeros((d, SSM_PROJ_DIM - w_in.shape[1]), w_in.dtype)], axis=1).astype(BF16)
            w_out = ssm_w_out[j].astype(BF16)
            pad = SSM_DT_PAD - 2 * SSM_HEADS
            dtb_row = jnp.pad(ssm_dt_bias[j].reshape(1, -1), ((0, 0), (0, pad)))
            alog_row = jnp.pad(ssm_a_log[j].reshape(1, -1), ((0, 0), (0, pad)))
            d_row = jnp.repeat(ssm_d[j], SSM_HEAD_DIM).reshape(1, SSM_D_INNER)
            outs = []
            for (x, cond_fn), (batch, seq, st0) in zip(
                    ((xp, _prompt_cond), (xs, sample_cond)),
                    ((bp, lp, None), (bs, ls, state_ssm[:, j]))):
                (proj,) = _norm_mod_matmul(x, norm1_w[i], mods, w_in_p, shift_idx=0, cond_of_block=cond_fn,
                                           tb=512, tn=896, out_dtypes=[F32], name="ssd_in_proj")
                xbc = _conv_silu(proj, ssm_conv_w[j], ssm_conv_b[j], batch, seq)
                want_state = st0 is None
                rf = _ssd_scan(xbc, proj, dtb_row, alog_row, st0, batch, seq, direction=0,
                               want_state=want_state)
                rb = _ssd_scan(xbc, proj, dtb_row, alog_row, st0, batch, seq, direction=1,
                               want_state=want_state)
                if want_state:
                    new_state = jnp.stack([rf[1], rb[1]], axis=1)[:, None]
                x_new = _ssd_output(rf[0], rb[0], xbc, proj, d_row, ssm_norm_w[j], w_out, x, mods,
                                    gate_idx=2, cond_of_block=cond_fn, tb=256)
                outs.append(x_new)
            xp, xs = outs
        wq = peer_w_q[i].astype(BF16)
        keys = peer_keys[i].astype(BF16)
        u = peer_u[i].astype(BF16)
        vt = peer_v[i].T.astype(BF16)
        xp = _peer_block(xp, norm2_w[i], mods, wq, keys, u, vt, _prompt_cond)
        xs = _peer_block(xs, norm2_w[i], mods, wq, keys, u, vt, sample_cond)
        streams = [(xp, _prompt_cond), (xs, sample_cond)]

    y_prompt = _final_rmsnorm(streams[0][0], final_norm_w).reshape(bp, lp, d)
    y_sample = _final_rmsnorm(streams[1][0], final_norm_w).reshape(bs, ls, d)
    return (y_prompt, y_sample, new_k, new_v, new_state)
```

```python
import functools
import math

import jax
import jax.numpy as jnp
from jax import lax
from jax.experimental import pallas as pl
from jax.experimental.pallas import tpu as pltpu

F32 = jnp.float32
BF16 = jnp.bfloat16
HIGHEST = lax.Precision.HIGHEST

D_MODEL = 1024
DEPTH = 2
GRID_W = 64
RMS_EPS = 1e-6
MASK_VALUE = -1e30
NA_HEADS = 16
NA_HEAD_DIM = 64
NA_WIN_H = 8
NA_WIN_W = 16
SSM_D_INNER = 2048
SSM_HEAD_DIM = 64
SSM_HEADS = 32
SSM_GROUPS = 8
SSM_HPG = 4
SSM_STATE = 128
SSM_CONV = 4
SSM_CHUNK = 128
SSM_CONV_DIM = 4096
SSM_DT_PAD = 128
SSM_PROJ_DIM = SSM_D_INNER + SSM_CONV_DIM + SSM_DT_PAD
PEER_HEADS = 8
PEER_N_KEYS = 128
PEER_EXPERTS = PEER_N_KEYS * PEER_N_KEYS
PEER_TOPK = 16
PEER_QUERY_DIM = 256
PEER_HALF = 128
BF16_ROWS = 16
GATE_LANES = 256
MXU_LANES = 256

VMEM_LIMIT = 56 * 1024 * 1024


def _params(sem, vmem=VMEM_LIMIT, flags=None):
    return pltpu.CompilerParams(dimension_semantics=sem, vmem_limit_bytes=vmem, flags=flags)


def _nt_dot(a, b):
    return lax.dot_general(a, b, (((1,), (1,)), ((), ())), preferred_element_type=F32)


def _tn_dot(a, b):
    return lax.dot_general(a, b, (((0,), (0,)), ((), ())), preferred_element_type=F32)


def _silu(x):
    return x / (1.0 + jnp.exp(-x))


def _softplus(x):
    return jnp.maximum(x, 0.0) + jnp.log1p(jnp.exp(-jnp.abs(x)))


def _gelu_tanh(x):
    c = math.sqrt(2.0 / math.pi)
    half = 0.5 * x
    return half + half * jnp.tanh(x * (c + (c * 0.044715) * (x * x)))


def _mod_kernel(cond_ref, w_ref, b_ref, o_ref):
    s = _silu(cond_ref[...])
    o_ref[0] = jnp.dot(s, w_ref[0], preferred_element_type=F32, precision=HIGHEST) + b_ref[0]


def _modulation(cond, ada_w, ada_b):
    depth, d, n = ada_w.shape
    tn = 1536
    return pl.pallas_call(
        _mod_kernel,
        grid=(depth, n // tn),
        in_specs=[pl.BlockSpec((8, d), lambda i, j: (0, 0)),
                  pl.BlockSpec((1, d, tn), lambda i, j: (i, 0, j)),
                  pl.BlockSpec((1, 1, tn), lambda i, j: (i, 0, j))],
        out_specs=pl.BlockSpec((1, 8, tn), lambda i, j: (i, 0, j)),
        out_shape=jax.ShapeDtypeStruct((depth, 8, n), F32),
        compiler_params=_params(("arbitrary", "arbitrary")),
        name="adaln_modulation",
    )(cond, ada_w, ada_b.reshape(depth, 1, n))


def _nmm_kernel(x_ref, nw_ref, mod_ref, w_ref, *refs, shift_idx, n_out):
    out_refs, h_scr = refs[:n_out], refs[n_out]
    j = pl.program_id(1)

    @pl.when(j == 0)
    def _():
        x = x_ref[...]
        ms = jnp.mean(x * x, axis=-1, keepdims=True)
        xn = x * lax.rsqrt(ms + RMS_EPS) * nw_ref[...]
        m = mod_ref[0]
        h = xn * (1.0 + m[shift_idx + 1:shift_idx + 2]) + m[shift_idx:shift_idx + 1]
        h_scr[...] = h.astype(BF16)

    acc = jnp.dot(h_scr[...], w_ref[...], preferred_element_type=F32)
    return acc, out_refs, h_scr, j


def _nmm_kernel_main(x_ref, nw_ref, mod_ref, w_ref, *refs, shift_idx, out_dtypes, f32_tiles, emit_h):
    n_out = len(out_dtypes) + len(f32_tiles) + (1 if emit_h else 0)
    acc, out_refs, h_scr, j = _nmm_kernel(x_ref, nw_ref, mod_ref, w_ref, *refs,
                                          shift_idx=shift_idx, n_out=n_out)
    for r, dt in zip(out_refs, out_dtypes):
        r[...] = acc.astype(dt)
    for r, tile in zip(out_refs[len(out_dtypes):], f32_tiles):
        @pl.when(j == tile)
        def _(r=r):
            r[...] = acc
    if emit_h:
        @pl.when(j == 0)
        def _():
            out_refs[-1][...] = h_scr[...].astype(F32).T.astype(BF16)


def _norm_mod_matmul(x, norm_w, mods, w_bf16, *, shift_idx, cond_of_block, tb, tn,
                     out_dtypes, f32_tiles=(), emit_h=False, name="norm_mod_matmul"):
    t, d = x.shape
    n = w_bf16.shape[1]
    assert t % tb == 0 and n % tn == 0
    out_shape = [jax.ShapeDtypeStruct((t, n), dt) for dt in out_dtypes]
    out_specs = [pl.BlockSpec((tb, tn), lambda i, j: (i, j)) for _ in out_dtypes]
    for _ in f32_tiles:
        out_shape.append(jax.ShapeDtypeStruct((t, tn), F32))
        out_specs.append(pl.BlockSpec((tb, tn), lambda i, j: (i, 0)))
    if emit_h:
        out_shape.append(jax.ShapeDtypeStruct((d, t), BF16))
        out_specs.append(pl.BlockSpec((d, tb), lambda i, j: (0, i)))
    kern = functools.partial(_nmm_kernel_main, shift_idx=shift_idx, out_dtypes=tuple(out_dtypes),
                             f32_tiles=tuple(f32_tiles), emit_h=emit_h)
    return pl.pallas_call(
        kern,
        grid=(t // tb, n // tn),
        in_specs=[pl.BlockSpec((tb, d), lambda i, j: (i, 0)),
                  pl.BlockSpec((1, d), lambda i, j: (0, 0)),
                  pl.BlockSpec((1, 6, d), lambda i, j: (cond_of_block(i, tb), 0, 0)),
                  pl.BlockSpec((d, tn), lambda i, j: (0, j))],
        out_specs=out_specs,
        out_shape=out_shape,
        scratch_shapes=[pltpu.VMEM((tb, d), BF16)],
        compiler_params=_params(("arbitrary", "arbitrary")),
        name=name,
    )(x, norm_w.reshape(1, d), mods, w_bf16)


def _mm_res_kernel(a_ref, w_ref, x_ref, mod_ref, o_ref, *, gate_idx):
    acc = jnp.dot(a_ref[...], w_ref[...], preferred_element_type=F32)
    o_ref[...] = x_ref[...] + mod_ref[0][gate_idx:gate_idx + 1] * acc


def _matmul_residual(a, w_bf16, x, mods, *, gate_idx, cond_of_block, tb, name="matmul_residual"):
    t, k = a.shape
    d = w_bf16.shape[1]
    return pl.pallas_call(
        functools.partial(_mm_res_kernel, gate_idx=gate_idx),
        grid=(t // tb,),
        in_specs=[pl.BlockSpec((tb, k), lambda i: (i, 0)),
                  pl.BlockSpec((k, d), lambda i: (0, 0)),
                  pl.BlockSpec((tb, d), lambda i: (i, 0)),
                  pl.BlockSpec((1, 6, d), lambda i: (cond_of_block(i, tb), 0, 0))],
        out_specs=pl.BlockSpec((tb, d), lambda i: (i, 0)),
        out_shape=jax.ShapeDtypeStruct((t, d), F32),
        compiler_params=_params(("arbitrary",)),
        name=name,
    )(a, w_bf16, x, mods)


def _ctx_attn_kernel(q_ref, k_ref, v_ref, o_ref):
    scale = NA_HEAD_DIM ** -0.5
    for h in range(NA_HEADS):
        sl = slice(h * NA_HEAD_DIM, (h + 1) * NA_HEAD_DIM)
        s = _nt_dot(q_ref[:, sl], k_ref[:, sl]) * scale
        m = jnp.max(s, axis=-1, keepdims=True)
        p = jnp.exp(s - m)
        l = jnp.sum(p, axis=-1, keepdims=True)
        o = jnp.dot(p.astype(BF16), v_ref[:, sl], preferred_element_type=F32) / l
        o_ref[:, sl] = o.astype(o_ref.dtype)


def _context_attention(qkv_bf16, batch, seq):
    d = D_MODEL
    return pl.pallas_call(
        _ctx_attn_kernel,
        grid=(batch,),
        in_specs=[pl.BlockSpec((seq, d), lambda b: (b, 0)),
                  pl.BlockSpec((seq, d), lambda b: (b, 1)),
                  pl.BlockSpec((seq, d), lambda b: (b, 2))],
        out_specs=pl.BlockSpec((seq, d), lambda b: (b, 0)),
        out_shape=jax.ShapeDtypeStruct((batch * seq, d), BF16),
        compiler_params=_params(("arbitrary",)),
        name="context_attention",
    )(qkv_bf16, qkv_bf16, qkv_bf16)


def _na_window_start(r, rows):
    return jnp.clip(r - NA_WIN_H // 2, 0, rows - NA_WIN_H)


def _na_kernel(q_ref, k_ref, v_ref, ck_ref, cv_ref, bias_ref, o_ref, *, rows):
    scale = NA_HEAD_DIM ** -0.5
    r = pl.program_id(1)
    n_loc = NA_WIN_H * GRID_W
    start = pl.multiple_of(_na_window_start(r, rows) * GRID_W, GRID_W)
    pw = 2 * NA_HEAD_DIM
    ci = lax.broadcasted_iota(jnp.int32, (2 * GRID_W, pw), 0)
    di = lax.broadcasted_iota(jnp.int32, (2 * GRID_W, pw), 1)
    same_head = (ci < GRID_W) == (di < NA_HEAD_DIM)
    lane_first = lax.broadcasted_iota(jnp.int32, (GRID_W, pw), 1) < NA_HEAD_DIM
    for pair in range(NA_HEADS // 2):
        ps = slice(pair * pw, (pair + 1) * pw)
        q2 = q_ref[:, ps]
        qq = jnp.concatenate([q2, q2], axis=0)
        qq = jnp.where(same_head, qq, jnp.zeros_like(qq))
        v_loc = v_ref[pl.ds(start, n_loc), ps]
        bias = bias_ref[pair]
        s_loc = _nt_dot(k_ref[pl.ds(start, n_loc), ps], qq) * scale
        s_loc = jnp.where(bias > 0.5 * MASK_VALUE, s_loc + bias, MASK_VALUE)
        s_ctx = _nt_dot(ck_ref[:, ps], qq) * scale
        m = jnp.maximum(jnp.max(s_loc, axis=0, keepdims=True), jnp.max(s_ctx, axis=0, keepdims=True))
        p_loc = jnp.exp(s_loc - m)
        p_ctx = jnp.exp(s_ctx - m)
        inv_l = 1.0 / (jnp.sum(p_loc, axis=0, keepdims=True) + jnp.sum(p_ctx, axis=0, keepdims=True))
        o2 = (_tn_dot((p_loc * inv_l).astype(BF16), v_loc)
              + _tn_dot((p_ctx * inv_l).astype(BF16), cv_ref[:, ps]))
        o_ref[:, ps] = jnp.where(lane_first, o2[:GRID_W], o2[GRID_W:]).astype(o_ref.dtype)


def _na_bias_tables(rpb, rows):
    cols = jnp.arange(GRID_W)
    col_start = jnp.clip(cols - NA_WIN_W // 2, 0, GRID_W - NA_WIN_W)
    col_ok = (cols[None, :] >= col_start[:, None]) & (cols[None, :] < col_start[:, None] + NA_WIN_W)
    dc = jnp.clip(cols[None, :] - cols[:, None], -(NA_WIN_W - 1), NA_WIN_W - 1) + (NA_WIN_W - 1)
    full = rpb[:, :, dc].astype(F32)
    full = jnp.where(col_ok[None, None], full, MASK_VALUE)
    variants = []
    for v in range(NA_WIN_H):
        t = full[:, v:v + NA_WIN_H].reshape(NA_HEADS // 2, 2, NA_WIN_H, GRID_W, GRID_W)
        variants.append(jnp.transpose(t, (0, 2, 4, 1, 3)).reshape(NA_HEADS // 2, NA_WIN_H * GRID_W, 2 * GRID_W))
    return jnp.stack(variants, axis=0)


def _na_latent_attention(qkv_bf16, ck_bf16, cv_bf16, bias_tab, batch, n_tok):
    d = D_MODEL
    rows = n_tok // GRID_W
    past = ck_bf16.shape[1]
    n_loc = NA_WIN_H * GRID_W

    def variant(b, r):
        return _na_window_start(r, rows) - r + (NA_WIN_H - 1)

    return pl.pallas_call(
        functools.partial(_na_kernel, rows=rows),
        grid=(batch, rows),
        in_specs=[pl.BlockSpec((GRID_W, d), lambda b, r: (b * rows + r, 0)),
                  pl.BlockSpec((n_tok, d), lambda b, r: (b, 1)),
                  pl.BlockSpec((n_tok, d), lambda b, r: (b, 2)),
                  pl.BlockSpec((None, past, d), lambda b, r: (b, 0, 0)),
                  pl.BlockSpec((None, past, d), lambda b, r: (b, 0, 0)),
                  pl.BlockSpec((None, NA_HEADS // 2, n_loc, 2 * GRID_W), lambda b, r: (variant(b, r), 0, 0, 0))],
        out_specs=pl.BlockSpec((GRID_W, d), lambda b, r: (b * rows + r, 0)),
        out_shape=jax.ShapeDtypeStruct((batch * n_tok, d), BF16),
        compiler_params=_params(("arbitrary", "arbitrary")),
        name="neighbourhood_attention",
    )(qkv_bf16, qkv_bf16, qkv_bf16, ck_bf16, cv_bf16, bias_tab)


def _conv_silu_kernel(x_ref, w_ref, b_ref, o_ref):
    x = x_ref[...]
    n = x.shape[0]
    row = lax.broadcasted_iota(jnp.int32, x.shape, 0)
    w = w_ref[...]
    xm1 = jnp.where(row >= 1, pltpu.roll(x, 1, 0), 0.0)
    xp1 = jnp.where(row < n - 1, pltpu.roll(x, n - 1, 0), 0.0)
    xp2 = jnp.where(row < n - 2, pltpu.roll(x, n - 2, 0), 0.0)
    y = w[0:1] * xm1 + w[1:2] * x + w[2:3] * xp1 + w[3:4] * xp2 + b_ref[...]
    o_ref[...] = _silu(y)


def _conv_silu(proj, conv_w, conv_b, batch, seq):
    cb = 512
    c0 = SSM_D_INNER // cb
    return pl.pallas_call(
        _conv_silu_kernel,
        grid=(batch, SSM_CONV_DIM // cb),
        in_specs=[pl.BlockSpec((seq, cb), lambda b, j: (b, c0 + j)),
                  pl.BlockSpec((SSM_CONV, cb), lambda b, j: (0, j)),
                  pl.BlockSpec((1, cb), lambda b, j: (0, j))],
        out_specs=pl.BlockSpec((seq, cb), lambda b, j: (b, j)),
        out_shape=jax.ShapeDtypeStruct((batch * seq, SSM_CONV_DIM), F32),
        compiler_params=_params(("arbitrary", "arbitrary")),
        name="ssd_conv_silu",
    )(proj, conv_w, conv_b.reshape(1, SSM_CONV_DIM))


def _ssd_scan_kernel(*refs, direction, zero_init, want_state):
    it = iter(refs)
    x_ref, b_ref, c_ref, dt_ref, dtb_ref, alog_ref = (next(it) for _ in range(6))
    h0_ref = None if zero_init else next(it)
    y_ref = next(it)
    hl_ref = next(it) if want_state else None
    h_scr = next(it)

    q = SSM_CHUNK
    c = pl.program_id(1)
    nc = pl.num_programs(1)

    @pl.when(c == 0)
    def _():
        if zero_init:
            h_scr[...] = jnp.zeros_like(h_scr)
        else:
            h_scr[...] = h0_ref[...]

    dt_all = _softplus(dt_ref[...] + dtb_ref[...])
    da_all = dt_all * (-jnp.exp(alog_ref[...]))
    ri = lax.broadcasted_iota(jnp.int32, (q, q), 0)
    ci = lax.broadcasted_iota(jnp.int32, (q, q), 1)
    causal = (ci <= ri) if direction == 0 else (ci >= ri)
    cum = jnp.dot(jnp.where(causal, 1.0, 0.0).astype(F32), da_all,
                  preferred_element_type=F32, precision=HIGHEST)
    cum_t = cum.T
    dt_t = dt_all.T
    end_row = q - 1 if direction == 0 else 0
    e_end = jnp.exp(cum[end_row:end_row + 1, :])
    dt_end_t = dt_t * jnp.exp(cum_t[:, end_row:end_row + 1] - cum_t)

    for g in range(SSM_GROUPS):
        gs = slice(g * SSM_STATE, (g + 1) * SSM_STATE)
        bm = b_ref[:, gs].astype(BF16)
        cm32 = c_ref[:, gs]
        cb = _nt_dot(cm32.astype(BF16), bm)
        x_t = [x_ref[:, (g * SSM_HPG + 2 * k) * SSM_HEAD_DIM:(g * SSM_HPG + 2 * k + 2) * SSM_HEAD_DIM].T
               for k in range(SSM_HPG // 2)]
        for r in range(SSM_HPG):
            hd = g * SSM_HPG + r
            j = direction * SSM_HEADS + hd
            sl = slice(hd * SSM_HEAD_DIM, (hd + 1) * SSM_HEAD_DIM)
            cum_b = jnp.broadcast_to(cum[:, j:j + 1], (q, q))
            seg = cum_b - cum_t[j:j + 1, :]
            decay = jnp.where(causal, jnp.exp(jnp.where(causal, seg, 0.0)), 0.0)
            w_ts = cb * decay * dt_t[j:j + 1, :]
            x_r = x_ref[:, sl]
            h_r = h_scr[hd]
            c_dec = (cm32 * jnp.exp(cum_b)).astype(BF16)
            y_ref[:, sl] = (jnp.dot(w_ts.astype(BF16), x_r.astype(BF16), preferred_element_type=F32)
                            + _nt_dot(c_dec, h_r.astype(BF16)))
            x_r_t = x_t[r // 2][(r % 2) * SSM_HEAD_DIM:(r % 2 + 1) * SSM_HEAD_DIM]
            x_sc_t = x_r_t * dt_end_t[j:j + 1, :]
            h_scr[hd] = e_end[:, j:j + 1] * h_r + jnp.dot(x_sc_t.astype(BF16), bm, preferred_element_type=F32)

    if want_state:
        @pl.when(c == nc - 1)
        def _():
            hl_ref[...] = h_scr[...]


def _ssd_scan(xbc, proj, dt_bias_row, a_log_row, state0, batch, seq, *, direction, want_state):
    nc = seq // SSM_CHUNK
    q = SSM_CHUNK
    zero_init = state0 is None
    gn = SSM_GROUPS * SSM_STATE
    b_col = SSM_D_INNER // gn
    dt_col = (SSM_D_INNER + SSM_CONV_DIM) // SSM_DT_PAD

    def chunk(c):
        return c if direction == 0 else nc - 1 - c

    in_specs = [pl.BlockSpec((q, SSM_D_INNER), lambda b, c: (b * nc + chunk(c), 0)),
                pl.BlockSpec((q, gn), lambda b, c: (b * nc + chunk(c), b_col)),
                pl.BlockSpec((q, gn), lambda b, c: (b * nc + chunk(c), b_col + 1)),
                pl.BlockSpec((q, SSM_DT_PAD), lambda b, c: (b * nc + chunk(c), dt_col)),
                pl.BlockSpec((1, SSM_DT_PAD), lambda b, c: (0, 0)),
                pl.BlockSpec((1, SSM_DT_PAD), lambda b, c: (0, 0))]
    args = [xbc, xbc, xbc, proj, dt_bias_row, a_log_row]
    if not zero_init:
        in_specs.append(pl.BlockSpec((None, None, SSM_HEADS, SSM_HEAD_DIM, SSM_STATE),
                                     lambda b, c: (b, direction, 0, 0, 0)))
        args.append(state0)
    out_specs = [pl.BlockSpec((q, SSM_D_INNER), lambda b, c: (b * nc + chunk(c), 0))]
    out_shape = [jax.ShapeDtypeStruct((batch * seq, SSM_D_INNER), F32)]
    if want_state:
        out_specs.append(pl.BlockSpec((None, SSM_HEADS, SSM_HEAD_DIM, SSM_STATE), lambda b, c: (b, 0, 0, 0)))
        out_shape.append(jax.ShapeDtypeStruct((batch, SSM_HEADS, SSM_HEAD_DIM, SSM_STATE), F32))
    return pl.pallas_call(
        functools.partial(_ssd_scan_kernel, direction=direction, zero_init=zero_init, want_state=want_state),
        grid=(batch, nc),
        in_specs=in_specs,
        out_specs=out_specs,
        out_shape=out_shape,
        scratch_shapes=[pltpu.VMEM((SSM_HEADS, SSM_HEAD_DIM, SSM_STATE), F32)],
        compiler_params=_params(("arbitrary", "arbitrary")),
        name="ssd_scan_fwd" if direction == 0 else "ssd_scan_bwd",
    )(*args)


def _ssd_out_kernel(yf_ref, yb_ref, xs_ref, z_ref, d_ref, nw_ref, w_ref, x_ref, mod_ref, o_ref, y_scr,
                    *, gate_idx):
    y = yf_ref[...] + yb_ref[...] + d_ref[...] * xs_ref[...]
    y = y * _silu(z_ref[...])
    gw = SSM_D_INNER // SSM_GROUPS
    for gi in range(SSM_GROUPS):
        sl = slice(gi * gw, (gi + 1) * gw)
        yg = y[:, sl]
        ms = jnp.mean(yg * yg, axis=-1, keepdims=True)
        y_scr[:, sl] = (yg * lax.rsqrt(ms + RMS_EPS) * nw_ref[:, sl]).astype(BF16)
    acc = jnp.dot(y_scr[...], w_ref[...], preferred_element_type=F32)
    o_ref[...] = x_ref[...] + mod_ref[0][gate_idx:gate_idx + 1] * acc


def _ssd_output(y_f, y_b, xbc, proj, d_row, norm_w, w_out_bf16, x, mods, *, gate_idx, cond_of_block, tb):
    t, d = x.shape
    di = SSM_D_INNER
    return pl.pallas_call(
        functools.partial(_ssd_out_kernel, gate_idx=gate_idx),
        grid=(t // tb,),
        in_specs=[pl.BlockSpec((tb, di), lambda i: (i, 0)),
                  pl.BlockSpec((tb, di), lambda i: (i, 0)),
                  pl.BlockSpec((tb, di), lambda i: (i, 0)),
                  pl.BlockSpec((tb, di), lambda i: (i, 0)),
                  pl.BlockSpec((1, di), lambda i: (0, 0)),
                  pl.BlockSpec((1, di), lambda i: (0, 0)),
                  pl.BlockSpec((di, d), lambda i: (0, 0)),
                  pl.BlockSpec((tb, d), lambda i: (i, 0)),
                  pl.BlockSpec((1, 6, d), lambda i: (cond_of_block(i, tb), 0, 0))],
        out_specs=pl.BlockSpec((tb, d), lambda i: (i, 0)),
        out_shape=jax.ShapeDtypeStruct((t, d), F32),
        scratch_shapes=[pltpu.VMEM((tb, di), BF16)],
        compiler_params=_params(("arbitrary",)),
        name="ssd_gate_norm_out",
    )(y_f, y_b, xbc, proj, d_row, norm_w.reshape(1, di), w_out_bf16, x, mods)


def _oddeven_merge(lo, hi, r):
    step = r * 2
    if step < hi - lo:
        yield from _oddeven_merge(lo, hi, step)
        yield from _oddeven_merge(lo + r, hi, step)
        yield from [(i, i + r) for i in range(lo + r, hi - r, step)]
    else:
        yield (lo, lo + r)


def _oddeven_merge_sort(lo, hi):
    if hi - lo >= 1:
        mid = lo + (hi - lo) // 2
        yield from _oddeven_merge_sort(lo, mid)
        yield from _oddeven_merge_sort(mid + 1, hi)
        yield from _oddeven_merge(lo, hi, 1)


_SORT16 = tuple(_oddeven_merge_sort(0, PEER_TOPK - 1))
SUBLANES = 8


def _exchange(vals, i, j):
    hi, lo = jnp.maximum(vals[i], vals[j]), jnp.minimum(vals[i], vals[j])
    vals[i], vals[j] = hi, lo


def _top16_rows(s):
    k = PEER_TOPK
    tiles = [s[v * SUBLANES:(v + 1) * SUBLANES] for v in range(k)]
    for i, j in _SORT16:
        _exchange(tiles, i, j)
    for shift in (4, 2, 1):
        tiles = [jnp.maximum(tiles[v], pltpu.roll(tiles[k - 1 - v], shift, 0)) for v in range(k)]
        for dist in (8, 4, 2, 1):
            for i in range(k):
                if not i & dist:
                    _exchange(tiles, i, i + dist)
    return [t[0:1] for t in tiles]


def _stack_rows(rows, n):
    t = rows[0].shape[1]
    ri = lax.broadcasted_iota(jnp.int32, (n, t), 0)
    out = jnp.zeros((n, t), F32)
    for j, row in enumerate(rows):
        out = jnp.where(ri == j, row, out)
    return out


def _peer_topk_kernel(q_ref, keys_ref, r2_ref, e2_ref, n_ref, c_ref):
    k = PEER_TOPK
    tb = q_ref.shape[0]
    ri = lax.broadcasted_iota(jnp.int32, (k, tb), 0)

    s1 = _nt_dot(keys_ref[0], q_ref[:, :PEER_HALF])
    s2 = _nt_dot(keys_ref[1], q_ref[:, PEER_HALF:])
    v1 = _top16_rows(s1)
    v2 = _top16_rows(s2)
    v1a = _stack_rows(v1, k)
    v2a = _stack_rows(v2, k)
    r8 = lax.broadcasted_iota(jnp.int32, (8, tb), 0)
    v1lo, v1hi, v2lo = v1a[0:8], v1a[8:16], v2a[0:8]
    ninf = -jnp.inf
    blocks = [v1[0] + v2a,
              v1[1] + v2lo,
              jnp.where(r8 < 5, v1[2] + v2lo, ninf),
              jnp.where(r8 < 4, v1[3] + v2lo, ninf),
              jnp.where(r8 >= 4, v1lo + v2[0], ninf),
              v1hi + v2[0],
              jnp.where(r8 >= 4, v1lo + v2[1], ninf),
              jnp.where(r8 == 4, v1lo + v2[2], ninf)]
    cand = jnp.concatenate(blocks, axis=0)
    work = cand
    tau = None
    for _ in range(k):
        tau = jnp.max(work, axis=0, keepdims=True)
        work = jnp.where(work == tau, -jnp.inf, work)
    top = v1[0] + v2[0]
    chosen = cand >= tau
    z = jnp.sum(jnp.where(chosen, jnp.exp(jnp.where(chosen, cand, top) - top), 0.0),
                axis=0, keepdims=True)
    cnt = jnp.where(chosen, 1.0, 0.0)
    n_lo = cnt[40:48] + cnt[56:64] + cnt[64:72]
    for i, rows in enumerate((slice(0, 16), slice(16, 24), slice(24, 32), slice(32, 40))):
        n_lo = jnp.where(r8 == i, jnp.sum(cnt[rows], axis=0, keepdims=True), n_lo)
    n16 = jnp.concatenate([n_lo, cnt[48:56]], axis=0)
    c16 = jnp.exp(v1a - v1[0]) / z
    n_a = jnp.zeros_like(s1)
    c_a = jnp.zeros_like(s1)
    for i in range(k):
        hit = s1 == v1[i]
        n_a = jnp.where(hit, n16[i:i + 1], n_a)
        c_a = jnp.where(hit, c16[i:i + 1], c_a)
    rank2 = jnp.zeros_like(s2)
    for j in range(k):
        rank2 = rank2 + jnp.where(v2[j] > s2, 1.0, 0.0)
    r2_ref[...] = rank2.astype(BF16)
    e2_ref[...] = jnp.exp(s2 - v2[0]).astype(BF16)
    n_ref[...] = n_a
    c_ref[...] = c_a


def _peer_topk(q_bf16, keys_bf16, tb):
    t = q_bf16.shape[0]
    shp = jax.ShapeDtypeStruct((PEER_HEADS, PEER_N_KEYS, t), F32)
    shp16 = jax.ShapeDtypeStruct((PEER_HEADS, PEER_N_KEYS, t), BF16)
    spec = pl.BlockSpec((None, PEER_N_KEYS, tb), lambda i, h: (h, 0, i))
    return pl.pallas_call(
        _peer_topk_kernel,
        grid=(t // tb, PEER_HEADS),
        in_specs=[pl.BlockSpec((tb, PEER_QUERY_DIM), lambda i, h: (i, h)),
                  pl.BlockSpec((None, 2, PEER_N_KEYS, PEER_HALF), lambda i, h: (h, 0, 0, 0))],
        out_specs=[spec, spec, spec, spec],
        out_shape=[shp16, shp16, shp, shp],
        compiler_params=_params(("arbitrary", "arbitrary")),
        name="peer_product_key_topk",
    )(q_bf16, keys_bf16)


def _peer_gate_chunk(s, ls, ht_ref, p_ref, r2_ref, e2_ref, n_ref, c_ref):
    n_bt = PEER_N_KEYS // BF16_ROWS
    w = [None] * n_bt
    for h in range(PEER_HEADS):
        n_b = jnp.broadcast_to(n_ref[h, s:s + 1, ls], (BF16_ROWS, GATE_LANES)).astype(BF16)
        c_b = jnp.broadcast_to(c_ref[h, s:s + 1, ls], (BF16_ROWS, GATE_LANES)).astype(BF16)
        for i in range(n_bt):
            bs = slice(i * BF16_ROWS, (i + 1) * BF16_ROWS)
            term = jnp.where(r2_ref[h, bs, ls] < n_b, e2_ref[h, bs, ls] * c_b, jnp.zeros_like(c_b))
            w[i] = term if w[i] is None else w[i] + term
    for i in range(n_bt):
        rs = slice(s * PEER_N_KEYS + i * BF16_ROWS, s * PEER_N_KEYS + (i + 1) * BF16_ROWS)
        p_ref[rs, ls] = _gelu_tanh(ht_ref[rs, ls]).astype(BF16) * w[i]


def _peer_dense_kernel(ft_ref, u_ref, vt_ref, r2_ref, e2_ref, n_ref, c_ref, x_ref, mod_ref, o_ref,
                       acc_ref, p_ref, ht0_ref, ht1_ref, *, gate_idx, eb):
    e = pl.program_id(1)
    ne = pl.num_programs(1) - 1
    sub = eb // PEER_N_KEYS
    tb = ft_ref.shape[1]

    @pl.when(e == 0)
    def _():
        acc_ref[...] = jnp.zeros_like(acc_ref)
        ht1_ref[...] = jnp.zeros_like(ht1_ref)

    def step(ht_new, ht_old):
        ht_new[...] = jnp.dot(u_ref[...], ft_ref[...], preferred_element_type=F32)
        for s in range(sub):
            for l in range(tb // GATE_LANES):
                _peer_gate_chunk(s, slice(l * GATE_LANES, (l + 1) * GATE_LANES),
                                 ht_old, p_ref, r2_ref, e2_ref, n_ref, c_ref)
        acc_ref[...] += jnp.dot(vt_ref[...], p_ref[...], preferred_element_type=F32)

    for parity, bufs in enumerate(((ht0_ref, ht1_ref), (ht1_ref, ht0_ref))):
        pl.when(e % 2 == parity)(functools.partial(step, *bufs))

    @pl.when(e == ne)
    def _():
        o_ref[...] = x_ref[...] + mod_ref[0][gate_idx:gate_idx + 1] * acc_ref[...].T


def _peer_dense(f_bf16, u_bf16, vt_bf16, r2, e2, n_a, c_a, x, mods, *, gate_idx, cond_of_block, tb, eb):
    t, d = x.shape
    assert t % tb == 0 and tb % GATE_LANES == 0 and PEER_EXPERTS % eb == 0 and eb % PEER_N_KEYS == 0
    ne = PEER_EXPERTS // eb
    sub = eb // PEER_N_KEYS
    assert sub % 8 == 0
    aux = pl.BlockSpec((PEER_HEADS, PEER_N_KEYS, tb), lambda i, e: (0, 0, i))
    aux_a = pl.BlockSpec((PEER_HEADS, sub, tb), lambda i, e: (0, jnp.clip(e - 1, 0, ne - 1), i))
    return pl.pallas_call(
        functools.partial(_peer_dense_kernel, gate_idx=gate_idx, eb=eb),
        grid=(t // tb, ne + 1),
        in_specs=[pl.BlockSpec((d, tb), lambda i, e: (0, i)),
                  pl.BlockSpec((eb, d), lambda i, e: (jnp.minimum(e, ne - 1), 0)),
                  pl.BlockSpec((d, eb), lambda i, e: (0, jnp.maximum(e - 1, 0))),
                  aux, aux, aux_a, aux_a,
                  pl.BlockSpec((tb, d), lambda i, e: (i, 0)),
                  pl.BlockSpec((1, 6, d), lambda i, e: (cond_of_block(i, tb), 0, 0))],
        out_specs=pl.BlockSpec((tb, d), lambda i, e: (i, 0)),
        out_shape=jax.ShapeDtypeStruct((t, d), F32),
        scratch_shapes=[pltpu.VMEM((d, tb), F32), pltpu.VMEM((eb, tb), BF16),
                        pltpu.VMEM((eb, tb), F32), pltpu.VMEM((eb, tb), F32)],
        compiler_params=_params(("arbitrary", "arbitrary")),
        name="peer_dense_experts",
    )(f_bf16, u_bf16, vt_bf16, r2, e2, n_a, c_a, x, mods)


def _peer_block(x, norm_w, mods, wq_bf16, keys_bf16, u_bf16, vt_bf16, cond_of_block):
    q, f = _norm_mod_matmul(x, norm_w, mods, wq_bf16, shift_idx=3, cond_of_block=cond_of_block,
                            tb=512, tn=1024, out_dtypes=[BF16], emit_h=True, name="peer_query")
    r2, e2, n_a, c_a = _peer_topk(q, keys_bf16, tb=512)
    return _peer_dense(f, u_bf16, vt_bf16, r2, e2, n_a, c_a, x, mods, gate_idx=5,
                       cond_of_block=cond_of_block, tb=512, eb=1024)


def _rmsnorm_kernel(x_ref, w_ref, o_ref):
    x = x_ref[...]
    ms = jnp.mean(x * x, axis=-1, keepdims=True)
    o_ref[...] = x * lax.rsqrt(ms + RMS_EPS) * w_ref[...]


def _final_rmsnorm(x, w, tb=512):
    t, d = x.shape
    return pl.pallas_call(
        _rmsnorm_kernel,
        grid=(t // tb,),
        in_specs=[pl.BlockSpec((tb, d), lambda i: (i, 0)), pl.BlockSpec((1, d), lambda i: (0, 0))],
        out_specs=pl.BlockSpec((tb, d), lambda i: (i, 0)),
        out_shape=jax.ShapeDtypeStruct((t, d), F32),
        compiler_params=_params(("arbitrary",)),
        name="final_rmsnorm",
    )(x, w.reshape(1, d))


def _prompt_cond(i, tb):
    return 0


def _make_sample_cond(seq):
    def cond(i, tb):
        return 1 + (i * tb) // seq
    return cond


def kernel(x_prompt, x_sample, cache_k, cache_v, state_ssm, c, c_ctx, ada_w, ada_b, norm1_w, norm2_w,
           final_norm_w, na_w_qkv, na_w_o, na_rpb, ssm_w_in, ssm_conv_w, ssm_conv_b, ssm_dt_bias,
           ssm_a_log, ssm_d, ssm_norm_w, ssm_w_out, peer_w_q, peer_keys, peer_u, peer_v):
    bp, lp, d = x_prompt.shape
    bs, ls, _ = x_sample.shape
    sample_cond = _make_sample_cond(ls)
    streams = [(x_prompt.reshape(bp * lp, d), _prompt_cond), (x_sample.reshape(bs * ls, d), sample_cond)]

    cond = jnp.zeros((8, d), F32).at[0].set(c_ctx).at[1:1 + bs].set(c)
    mods_all = _modulation(cond, ada_w, ada_b).reshape(DEPTH, 8, 6, d)

    new_k = new_v = new_state = None
    for i in range(DEPTH):
        j = i // 2
        mods = mods_all[i]
        xp, xs = streams[0][0], streams[1][0]
        if i % 2 == 0:
            wqkv = na_w_qkv[j].astype(BF16)
            wo = na_w_o[j].astype(BF16)
            qkv_p16, k_p, v_p = _norm_mod_matmul(xp, norm1_w[i], mods, wqkv, shift_idx=0,
                                                 cond_of_block=_prompt_cond, tb=512, tn=d,
                                                 out_dtypes=[BF16], f32_tiles=(1, 2), name="na_qkv_prompt")
            (qkv_s16,) = _norm_mod_matmul(xs, norm1_w[i], mods, wqkv, shift_idx=0,
                                          cond_of_block=sample_cond, tb=512, tn=1024,
                                          out_dtypes=[BF16], name="na_qkv_sample")
            new_k = k_p.reshape(bp, 1, lp, NA_HEADS, NA_HEAD_DIM)
            new_v = v_p.reshape(bp, 1, lp, NA_HEADS, NA_HEAD_DIM)
            o_p = _context_attention(qkv_p16, bp, lp)
            past = cache_k.shape[2]
            ck = cache_k[:, j].reshape(bs, past, d).astype(BF16)
            cv = cache_v[:, j].reshape(bs, past, d).astype(BF16)
            bias_tab = _na_bias_tables(na_rpb[j], ls // GRID_W)
            o_s = _na_latent_attention(qkv_s16, ck, cv, bias_tab, bs, ls)
            xp = _matmul_residual(o_p, wo, xp, mods, gate_idx=2, cond_of_block=_prompt_cond, tb=512,
                                  name="na_out_prompt")
            xs = _matmul_residual(o_s, wo, xs, mods, gate_idx=2, cond_of_block=sample_cond, tb=512,
                                  name="na_out_sample")
        else:
            w_in = ssm_w_in[j]
            n_main = SSM_D_INNER + SSM_CONV_DIM
            w_in_p = jnp.concatenate(
                [w_in, jnp.zeros((d, SSM_PROJ_DIM - w_in.shape[1]), w_in.dtype)], axis=1).astype(BF16)
            w_out = ssm_w_out[j].astype(BF16)
            pad = SSM_DT_PAD - 2 * SSM_HEADS
            dtb_row = jnp.pad(ssm_dt_bias[j].reshape(1, -1), ((0, 0), (0, pad)))
            alog_row = jnp.pad(ssm_a_log[j].reshape(1, -1), ((0, 0), (0, pad)))
            d_row = jnp.repeat(ssm_d[j], SSM_HEAD_DIM).reshape(1, SSM_D_INNER)
            outs = []
            for (x, cond_fn), (batch, seq, st0) in zip(
                    ((xp, _prompt_cond), (xs, sample_cond)),
                    ((bp, lp, None), (bs, ls, state_ssm[:, j]))):
                (proj,) = _norm_mod_matmul(x, norm1_w[i], mods, w_in_p, shift_idx=0, cond_of_block=cond_fn,
                                           tb=512, tn=896, out_dtypes=[F32], name="ssd_in_proj")
                xbc = _conv_silu(proj, ssm_conv_w[j], ssm_conv_b[j], batch, seq)
                want_state = st0 is None
                rf = _ssd_scan(xbc, proj, dtb_row, alog_row, st0, batch, seq, direction=0,
                               want_state=want_state)
                rb = _ssd_scan(xbc, proj, dtb_row, alog_row, st0, batch, seq, direction=1,
                               want_state=want_state)
                if want_state:
                    new_state = jnp.stack([rf[1], rb[1]], axis=1)[:, None]
                x_new = _ssd_output(rf[0], rb[0], xbc, proj, d_row, ssm_norm_w[j], w_out, x, mods,
                                    gate_idx=2, cond_of_block=cond_fn, tb=256)
                outs.append(x_new)
            xp, xs = outs
        wq = peer_w_q[i].astype(BF16)
        keys = peer_keys[i].astype(BF16)
        u = peer_u[i].astype(BF16)
        vt = peer_v[i].T.astype(BF16)
        xp = _peer_block(xp, norm2_w[i], mods, wq, keys, u, vt, _prompt_cond)
        xs = _peer_block(xs, norm2_w[i], mods, wq, keys, u, vt, sample_cond)
        streams = [(xp, _prompt_cond), (xs, sample_cond)]

    y_prompt = _final_rmsnorm(streams[0][0], final_norm_w).reshape(bp, lp, d)
    y_sample = _final_rmsnorm(streams[1][0], final_norm_w).reshape(bs, ls, d)
    return (y_prompt, y_sample, new_k, new_v, new_state)
```
